```python
import math
import jax, jax.numpy as jnp
from jax import lax
import numpy as np

D_MODEL = 2048
BATCH = 8
SEQ = 4096
DEPTH = 2

H_A = 8
HD_A = 128
KV_RANK = 256
IDX_HEADS = 8
IDX_DIM = 64
TOPK_MAX = 256
H_B = 16
KVH_B = 2
GQA_GROUP = H_B // KVH_B
HD_B = 64
WINDOW = 128
BLOCK = 128
N_BUCKETS = 32
MAX_DISTANCE = 128
N_BIAS_HEADS = H_A + H_B
D_FF = 5632
EPS = 1e-6

W_QA = H_A * HD_A
W_CKV = KV_RANK
W_QI = IDX_HEADS * IDX_DIM
W_KI = IDX_DIM
W_WI = IDX_HEADS
W_QB = H_B * HD_B
W_KB = KVH_B * HD_B
W_VB = KVH_B * HD_B
D_IN = W_QA + W_CKV + W_QI + W_KI + W_WI + W_QB + W_KB + W_VB
D_MIX = H_A * HD_A + H_B * HD_B

kernel_name = "hymba_dsa_swa_macaron_t5"


def rmsnorm(x, g):
    xf = x.astype(jnp.float32)
    y = xf * lax.rsqrt(jnp.mean(xf * xf, axis=-1, keepdims=True) + EPS)
    return (y * g.astype(jnp.float32)).astype(x.dtype)


def layernorm(x, g, b):
    xf = x.astype(jnp.float32)
    mu = jnp.mean(xf, axis=-1, keepdims=True)
    var = jnp.mean(jnp.square(xf - mu), axis=-1, keepdims=True)
    y = (xf - mu) * lax.rsqrt(var + EPS)
    return (y * g.astype(jnp.float32) + b.astype(jnp.float32)).astype(x.dtype)


def swiglu(x, w_gate, w_up, w_down):
    return (jax.nn.silu(x @ w_gate) * (x @ w_up)) @ w_down


def t5_bucket(dist):
    n = jnp.maximum(dist, 0)
    max_exact = N_BUCKETS // 2
    nf = jnp.maximum(n, 1).astype(jnp.float32)
    large = max_exact + (jnp.log(nf / max_exact) / math.log(MAX_DISTANCE / max_exact)
                         * (N_BUCKETS - max_exact)).astype(jnp.int32)
    large = jnp.minimum(large, N_BUCKETS - 1)
    return jnp.where(n < max_exact, n, large)


def dsa_mixer(q, c_kv, q_idx, k_idx, w_idx, w_uk, w_uv, bias_table):
    b, s = q.shape[0], q.shape[1]
    nb = s // BLOCK
    k_sel = min(TOPK_MAX, s // 4)
    q_lat = jnp.einsum("bshd,rhd->bshr", q, w_uk)
    key_pos = jnp.arange(s)
    k_idx32 = k_idx.astype(jnp.float32)

    def to_blocks(a):
        return a.reshape((b, nb, BLOCK) + a.shape[2:]).swapaxes(0, 1)

    def block_fn(args):
        qi, ql, wi, blk = args
        t = blk * BLOCK + jnp.arange(BLOCK)
        dots = jnp.einsum("bthd,bsd->bths", qi.astype(jnp.float32), k_idx32)
        score = jnp.einsum("bth,bths->bts", wi.astype(jnp.float32), jax.nn.relu(dots))
        score = jnp.where((key_pos[None, :] <= t[:, None])[None], score, -jnp.inf)
        _, idx = lax.top_k(score, k_sel)
        c_sel = jax.vmap(lambda c, i: c[i])(c_kv, idx)
        dist = t[None, :, None] - idx
        bias = jnp.moveaxis(bias_table[t5_bucket(dist)], -1, 2).astype(jnp.float32)
        logits = jnp.einsum("bthr,btkr->bthk", ql, c_sel).astype(jnp.float32) * (HD_A ** -0.5) + bias
        logits = jnp.where((dist >= 0)[:, :, None, :], logits, -jnp.inf)
        p = jax.nn.softmax(logits, axis=-1).astype(c_sel.dtype)
        return jnp.einsum("bthk,btkr->bthr", p, c_sel)

    o_lat = lax.map(block_fn, (to_blocks(q_idx), to_blocks(q_lat), to_blocks(w_idx), jnp.arange(nb)))
    o_lat = o_lat.swapaxes(0, 1).reshape(b, s, H_A, KV_RANK)
    return jnp.einsum("bshr,rhd->bshd", o_lat, w_uv).reshape(b, s, H_A * HD_A)


def swa_mixer(q, k, v, sinks, bias_table):
    b, s = q.shape[0], q.shape[1]
    nb = s // BLOCK
    qb = q.reshape(b, nb, BLOCK, KVH_B, GQA_GROUP, HD_B).swapaxes(0, 1)

    def band(a):
        a = a.reshape(b, nb, BLOCK, KVH_B, HD_B)
        prev = jnp.concatenate([jnp.zeros_like(a[:, :1]), a[:, :-1]], axis=1)
        return jnp.concatenate([prev, a], axis=2).swapaxes(0, 1)

    kb, vb = band(k), band(v)
    qi_pos = jnp.arange(BLOCK)
    kj_pos = jnp.arange(2 * BLOCK)
    dist = BLOCK + qi_pos[:, None] - kj_pos[None, :]
    in_window = (dist >= 0) & (dist < WINDOW)
    bias = jnp.moveaxis(bias_table[t5_bucket(dist)], -1, 0)
    bias = bias.reshape(KVH_B, GQA_GROUP, BLOCK, 2 * BLOCK).astype(jnp.float32)
    sink = sinks.astype(jnp.float32).reshape(KVH_B, GQA_GROUP, 1, 1)

    def block_fn(args):
        qi, ki, vi, blk = args
        mask = in_window & (((blk - 1) * BLOCK + kj_pos) >= 0)[None, :]
        logits = jnp.einsum("btkgd,bskd->bkgts", qi, ki).astype(jnp.float32) * (HD_B ** -0.5) + bias
        logits = jnp.where(mask, logits, -jnp.inf)
        sink_col = jnp.broadcast_to(sink, logits.shape[:-1] + (1,))
        p = jax.nn.softmax(jnp.concatenate([logits, sink_col], axis=-1), axis=-1)[..., :-1]
        return jnp.einsum("bkgts,bskd->btkgd", p.astype(vi.dtype), vi)

    o = lax.map(block_fn, (qb, kb, vb, jnp.arange(nb)))
    return o.swapaxes(0, 1).reshape(b, s, H_B * HD_B)


def setup_inputs(seed: int = 0) -> dict:
    key = jax.random.key(seed)
    ks = jax.random.split(key, 24)
    f32 = jnp.float32

    def nrm(k, shape, scale):
        return jax.random.normal(k, shape, f32) * scale

    def gain(k, shape):
        return 1.0 + 0.01 * jax.random.normal(k, shape, f32)

    L = DEPTH
    return {
        "x": jax.random.normal(ks[0], (BATCH, SEQ, D_MODEL), f32),
        "rel_bias": nrm(ks[1], (N_BUCKETS, N_BIAS_HEADS), 0.3),
        "ffn1_norm": gain(ks[2], (L, D_MODEL)),
        "ffn1_gate": nrm(ks[3], (L, D_MODEL, D_FF), D_MODEL ** -0.5),
        "ffn1_up": nrm(ks[4], (L, D_MODEL, D_FF), D_MODEL ** -0.5),
        "ffn1_down": nrm(ks[5], (L, D_FF, D_MODEL), D_FF ** -0.5),
        "mix_norm": gain(ks[6], (L, D_MODEL)),
        "w_in": nrm(ks[7], (L, D_MODEL, D_IN), D_MODEL ** -0.5),
        "kv_norm": gain(ks[8], (L, KV_RANK)),
        "idx_k_norm_g": gain(ks[9], (L, IDX_DIM)),
        "idx_k_norm_b": nrm(ks[10], (L, IDX_DIM), 0.01),
        "w_uk": nrm(ks[11], (L, KV_RANK, H_A, HD_A), KV_RANK ** -0.5),
        "w_uv": nrm(ks[12], (L, KV_RANK, H_A, HD_A), KV_RANK ** -0.5),
        "sinks": nrm(ks[13], (L, H_B), 0.5),
        "w_out": nrm(ks[14], (L, D_MIX, D_MODEL), D_MIX ** -0.5),
        "ffn2_norm": gain(ks[15], (L, D_MODEL)),
        "ffn2_gate": nrm(ks[16], (L, D_MODEL, D_FF), D_MODEL ** -0.5),
        "ffn2_up": nrm(ks[17], (L, D_MODEL, D_FF), D_MODEL ** -0.5),
        "ffn2_down": nrm(ks[18], (L, D_FF, D_MODEL), D_FF ** -0.5),
        "final_norm": gain(ks[19], (D_MODEL,)),
    }


def reference(x, rel_bias, ffn1_norm, ffn1_gate, ffn1_up, ffn1_down, mix_norm, w_in, kv_norm,
              idx_k_norm_g, idx_k_norm_b, w_uk, w_uv, sinks, w_out, ffn2_norm, ffn2_gate,
              ffn2_up, ffn2_down, final_norm):
    b, s, _ = x.shape
    bias_a = rel_bias[:, :H_A]
    bias_b = rel_bias[:, H_A:]
    splits = [W_QA, W_CKV, W_QI, W_KI, W_WI, W_QB, W_KB]
    cuts = [int(c) for c in np.cumsum(splits)]
    h = x
    for l in range(DEPTH):
        h = h + 0.5 * swiglu(rmsnorm(h, ffn1_norm[l]), ffn1_gate[l], ffn1_up[l], ffn1_down[l])
        z = rmsnorm(h, mix_norm[l]) @ w_in[l]
        q_a, c_kv, q_i, k_i, w_i, q_b, k_b, v_b = jnp.split(z, cuts, axis=-1)
        q_a = q_a.reshape(b, s, H_A, HD_A)
        c_kv = rmsnorm(c_kv, kv_norm[l])
        q_i = q_i.reshape(b, s, IDX_HEADS, IDX_DIM)
        k_i = layernorm(k_i, idx_k_norm_g[l], idx_k_norm_b[l])
        w_i = w_i * (IDX_HEADS ** -0.5 * IDX_DIM ** -0.5)
        o_a = dsa_mixer(q_a, c_kv, q_i, k_i, w_i, w_uk[l], w_uv[l], bias_a)
        o_b = swa_mixer(q_b.reshape(b, s, H_B, HD_B), k_b.reshape(b, s, KVH_B, HD_B),
                        v_b.reshape(b, s, KVH_B, HD_B), sinks[l], bias_b)
        h = h + jnp.concatenate([o_a, o_b], axis=-1) @ w_out[l]
        h = h + 0.5 * swiglu(rmsnorm(h, ffn2_norm[l]), ffn2_gate[l], ffn2_up[l], ffn2_down[l])
    return rmsnorm(h, final_norm)
```

```python
import functools
import math

import numpy as np
import jax
import jax.numpy as jnp
from jax import lax
from jax.experimental import pallas as pl
from jax.experimental.pallas import tpu as pltpu

H_A = 8
HD_A = 128
KV_RANK = 256
IDX_HEADS = 8
IDX_DIM = 64
TOPK_MAX = 256
H_B = 16
KVH_B = 2
GQA_GROUP = H_B // KVH_B
HD_B = 64
WINDOW = 128
BLOCK = 128
N_BUCKETS = 32
MAX_DISTANCE = 128
EPS = 1e-6

LANES = 128
SUBLANES = 8
VMEM_LIMIT_BYTES = 56 * 1024 * 1024

FFN_TOKEN_TILE = 1024
FFN_HIDDEN_TILE = 512
PROJ_TOKEN_TILE = 512
OUT_TOKEN_TILE = 512
DSA_KEY_TILE = 256

NEG = -1e30
F32_LOWEST = float(np.finfo(np.float32).min)

W_QA = H_A * HD_A
W_QI = IDX_HEADS * IDX_DIM
W_QB = H_B * HD_B
W_KVB = KVH_B * HD_B
OFF_QA = 0
OFF_CKV = OFF_QA + W_QA
OFF_QI = OFF_CKV + KV_RANK
OFF_KK = OFF_QI + W_QI
OFF_WI = OFF_KK + 2 * IDX_DIM
OFF_QB = OFF_WI + LANES
OFF_KB = OFF_QB + W_QB
OFF_VB = OFF_KB + 2 * W_KVB
D_IN_PAD = OFF_VB + 2 * W_KVB


def _rms(x, g):
    return x * lax.rsqrt(jnp.mean(x * x, axis=-1, keepdims=True) + EPS) * g


def _ffn_kernel(h_ref, g_ref, wg_ref, wu_ref, wd_ref, fg_ref, o_ref, xn_ref, *, final_norm):
    j = pl.program_id(1)

    @pl.when(j == 0)
    def _():
        x = h_ref[...]
        xn_ref[...] = _rms(x, g_ref[...]).astype(jnp.bfloat16)
        o_ref[...] = x

    xn = xn_ref[...]
    gate = jnp.dot(xn, wg_ref[...], preferred_element_type=jnp.float32)
    up = jnp.dot(xn, wu_ref[...], preferred_element_type=jnp.float32)
    act = (gate * jax.nn.sigmoid(gate) * (0.5 * up)).astype(jnp.bfloat16)
    o_ref[...] += jnp.dot(act, wd_ref[...], preferred_element_type=jnp.float32)

    if final_norm:
        @pl.when(j == pl.num_programs(1) - 1)
        def _():
            o_ref[...] = _rms(o_ref[...], fg_ref[...])


def _ffn(h, g, wg, wu, wd, fg, final_norm):
    n, d = h.shape
    f = wg.shape[1]
    tm = min(FFN_TOKEN_TILE, n)
    tf = FFN_HIDDEN_TILE
    return pl.pallas_call(
        functools.partial(_ffn_kernel, final_norm=final_norm),
        grid=(n // tm, f // tf),
        in_specs=[
            pl.BlockSpec((tm, d), lambda i, j: (i, 0), pipeline_mode=pl.Buffered(1)),
            pl.BlockSpec((1, d), lambda i, j: (0, 0)),
            pl.BlockSpec((d, tf), lambda i, j: (0, j)),
            pl.BlockSpec((d, tf), lambda i, j: (0, j)),
            pl.BlockSpec((tf, d), lambda i, j: (j, 0)),
            pl.BlockSpec((1, d), lambda i, j: (0, 0)),
        ],
        out_specs=pl.BlockSpec((tm, d), lambda i, j: (i, 0)),
        out_shape=jax.ShapeDtypeStruct((n, d), jnp.float32),
        scratch_shapes=[pltpu.VMEM((tm, d), jnp.bfloat16)],
        compiler_params=pltpu.CompilerParams(
            dimension_semantics=("parallel", "arbitrary"),
            vmem_limit_bytes=VMEM_LIMIT_BYTES),
        name="ffn",
    )(h, g, wg, wu, wd, fg)


def _proj_kernel(h_ref, g_ref, w_ref, kvg_ref, kg_ref, kb_ref,
                 qa_o, ckv_o, ckvt_o, qi_o, kk_o, wi_o, qb_o, kb_o, vb_o, xn_ref):
    xn_ref[...] = _rms(h_ref[0], g_ref[...]).astype(jnp.bfloat16)

    def seg(lo, width):
        return jnp.dot(xn_ref[...], w_ref[:, lo:lo + width], preferred_element_type=jnp.float32)

    qa_o[0] = seg(OFF_QA, W_QA).astype(jnp.bfloat16)

    c = _rms(seg(OFF_CKV, KV_RANK), kvg_ref[...])
    ckv_o[0] = c.astype(jnp.bfloat16)
    ct = c.T
    tk = ckvt_o.shape[3]
    for t in range(ckvt_o.shape[1]):
        ckvt_o[0, t] = ct[:, t * tk:(t + 1) * tk].astype(jnp.bfloat16)

    qi_o[0] = seg(OFF_QI, W_QI).astype(jnp.bfloat16)

    k = seg(OFF_KK, 2 * IDX_DIM)
    mu = jnp.mean(k, axis=-1, keepdims=True)
    kc = k - mu
    var = jnp.mean(kc * kc, axis=-1, keepdims=True)
    kk_o[0] = (kc * lax.rsqrt(var + EPS) * kg_ref[...] + kb_ref[...]).astype(jnp.bfloat16)

    wi_o[0] = seg(OFF_WI, LANES) * (IDX_HEADS ** -0.5 * IDX_DIM ** -0.5)
    qb_o[0] = (seg(OFF_QB, W_QB) * (HD_B ** -0.5)).astype(jnp.bfloat16)
    kb_o[0] = seg(OFF_KB, 2 * W_KVB).astype(jnp.bfloat16)
    vb_o[0] = seg(OFF_VB, 2 * W_KVB).astype(jnp.bfloat16)


def _proj(h, g, w, kvg, kg, kb, tk):
    b, s, d = h.shape
    tm = min(PROJ_TOKEN_TILE, s)
    nt = tm // tk
    bf = jnp.bfloat16

    def tok(width, dtype):
        return (jax.ShapeDtypeStruct((b, s, width), dtype),
                pl.BlockSpec((1, tm, width), lambda bi, ti: (bi, ti, 0)))

    outs = [
        tok(W_QA, bf),
        tok(KV_RANK, bf),
        (jax.ShapeDtypeStruct((b, s // tk, KV_RANK, tk), bf),
         pl.BlockSpec((1, nt, KV_RANK, tk), lambda bi, ti: (bi, ti, 0, 0))),
        tok(W_QI, bf),
        tok(2 * IDX_DIM, bf),
        tok(LANES, jnp.float32),
        tok(W_QB, bf),
        tok(2 * W_KVB, bf),
        tok(2 * W_KVB, bf),
    ]
    const = lambda bi, ti: (0, 0)
    return pl.pallas_call(
        _proj_kernel,
        grid=(b, s // tm),
        in_specs=[
            pl.BlockSpec((1, tm, d), lambda bi, ti: (bi, ti, 0)),
            pl.BlockSpec((1, d), const),
            pl.BlockSpec((d, D_IN_PAD), const, pipeline_mode=pl.Buffered(1)),
            pl.BlockSpec((1, KV_RANK), const),
            pl.BlockSpec((1, 2 * IDX_DIM), const),
            pl.BlockSpec((1, 2 * IDX_DIM), const),
        ],
        out_specs=[o[1] for o in outs],
        out_shape=[o[0] for o in outs],
        scratch_shapes=[pltpu.VMEM((tm, d), bf)],
        compiler_params=pltpu.CompilerParams(
            dimension_semantics=("parallel", "parallel"),
            vmem_limit_bytes=VMEM_LIMIT_BYTES),
        name="proj",
    )(h, g, w, kvg, kg, kb)


def _sortable_to_f32(u):
    k = u ^ jnp.int32(-2 ** 31)
    bits = k ^ ((k >> 31) & jnp.int32(0x7FFFFFFF))
    return lax.bitcast_convert_type(bits, jnp.float32)


def _dsa_kernel(qa_ref, qi_ref, wi_ref, kk_ref, ckv_ref, ckvt_ref, wuk_ref, wuv_ref, bias_ref,
                o_ref, score_ref, acc_ref, m_ref, l_ref, qlat_ref, *, k_sel, tk, idx_bits):
    i = pl.program_id(1)
    blocks_per_tile = tk // BLOCK
    n_blk = i + 1
    n_tiles = i // blocks_per_tile + 1
    f32 = jnp.float32
    bf16 = jnp.bfloat16

    lane = lax.broadcasted_iota(jnp.int32, (BLOCK, LANES), 1)
    parts = []
    for h in range(IDX_HEADS):
        pair = qi_ref[0, :, (h // 2) * LANES:(h // 2 + 1) * LANES].astype(f32)
        keep = (lane < IDX_DIM) if h % 2 == 0 else (lane >= IDX_DIM)
        parts.append(jnp.where(keep, pair, 0.0).astype(bf16))
    qm = jnp.concatenate(parts, axis=0)
    wit = wi_ref[0].T

    t_pos = i * BLOCK + lax.broadcasted_iota(jnp.int32, (tk, LANES), 1)
    row = lax.broadcasted_iota(jnp.int32, (tk, LANES), 0)

    def score_body(kt, carry):
        base = pl.multiple_of(kt * tk, tk)
        d = lax.dot_general(kk_ref[0, pl.ds(base, tk), :], qm, (((1,), (1,)), ((), ())),
                            preferred_element_type=f32)
        sc = jnp.zeros((tk, LANES), f32)
        for h in range(IDX_HEADS):
            sc = sc + jnp.maximum(d[:, h * BLOCK:(h + 1) * BLOCK], 0.0) * wit[h:h + 1, :]
        sc = jnp.where(base + row <= t_pos, sc, -jnp.inf)
        score_ref[pl.ds(base, tk), :] = sc
        return carry

    lax.fori_loop(0, n_tiles, score_body, 0)

    def count(pred):
        def body(c, acc):
            base = pl.multiple_of(c * BLOCK, BLOCK)
            hit = jnp.where(pred(score_ref[pl.ds(base, BLOCK), :], base), 1, 0)
            return acc + hit.reshape(BLOCK // SUBLANES, SUBLANES, LANES).sum(axis=0)
        acc = lax.fori_loop(0, n_blk, body, jnp.zeros((SUBLANES, LANES), jnp.int32))
        return jnp.sum(acc, axis=0, keepdims=True)

    def bit_body(it, u):
        cand = u | lax.shift_left(jnp.int32(1), 31 - it)
        thr_c = _sortable_to_f32(cand)
        cnt = count(lambda blk, base: blk >= thr_c)
        return jnp.where(cnt >= k_sel, cand, u)

    u = lax.fori_loop(0, 32, bit_body, jnp.zeros((1, LANES), jnp.int32))
    thr = _sortable_to_f32(u)
    thr = jnp.where(thr >= F32_LOWEST, thr, F32_LOWEST)

    cnt_ge = count(lambda blk, base: blk >= thr)
    has_tie = jnp.max(jnp.where(cnt_ge > k_sel, 1.0, 0.0)) > 0.5

    @pl.when(has_tie)
    def _():
        need = k_sel - count(lambda blk, base: blk > thr)
        row_b = lax.broadcasted_iota(jnp.int32, (BLOCK, LANES), 0)

        def idx_body(it, p):
            cand = p | lax.shift_left(jnp.int32(1), idx_bits - 1 - it)
            cnt = count(lambda blk, base: (blk == thr) & (base + row_b < cand))
            return jnp.where(cnt < need, cand, p)

        last = lax.fori_loop(0, idx_bits, idx_body, jnp.zeros((1, LANES), jnp.int32))

        def drop_body(c, carry):
            base = pl.multiple_of(c * BLOCK, BLOCK)
            blk = score_ref[pl.ds(base, BLOCK), :]
            score_ref[pl.ds(base, BLOCK), :] = jnp.where(
                (blk == thr) & (base + row_b > last), -jnp.inf, blk)
            return carry

        lax.fori_loop(0, n_blk, drop_body, 0)

    scale = HD_A ** -0.5
    for h in range(H_A):
        hs = slice(h * HD_A, (h + 1) * HD_A)
        qlt = lax.dot_general(wuk_ref[:, hs], qa_ref[0, :, hs], (((1,), (1,)), ((), ())),
                              preferred_element_type=f32)
        qlat_ref[:, h * BLOCK:(h + 1) * BLOCK] = (qlt * scale).astype(bf16)

    m_ref[...] = jnp.full(m_ref.shape, NEG, f32)
    l_ref[...] = jnp.zeros(l_ref.shape, f32)
    acc_ref[...] = jnp.zeros(acc_ref.shape, f32)

    def attend(kt, with_bias):
        base = pl.multiple_of(kt * tk, tk)
        lg = jnp.dot(ckv_ref[0, pl.ds(base, tk), :], qlat_ref[...],
                     preferred_element_type=f32)
        if with_bias:
            off = jnp.maximum(tk + base - (i - 1) * BLOCK, 0)
            lg = lg + bias_ref[pl.ds(pl.multiple_of(off, BLOCK), tk), :]
        keep = score_ref[pl.ds(base, tk), :] >= thr
        m_old = m_ref[...]
        ps, ms = [], []
        for h in range(H_A):
            hs = slice(h * BLOCK, (h + 1) * BLOCK)
            x = jnp.where(keep, lg[:, hs], NEG)
            m_new = jnp.maximum(m_old[:, hs], jnp.max(x, axis=0, keepdims=True))
            ps.append(jnp.exp(x - m_new))
            ms.append(m_new)
        p = jnp.concatenate(ps, axis=1)
        m_new = jnp.concatenate(ms, axis=1)
        alpha = jnp.exp(m_old - m_new)
        l_ref[...] = alpha * l_ref[...] + jnp.sum(p, axis=0, keepdims=True)
        acc_ref[...] = alpha * acc_ref[...] + jnp.dot(
            ckvt_ref[0, kt], p.astype(bf16), preferred_element_type=f32)
        m_ref[...] = m_new

    def far_body(kt, carry):
        attend(kt, False)
        return carry

    lax.fori_loop(0, jnp.maximum(n_tiles - 2, 0), far_body, 0)

    @pl.when(n_tiles >= 2)
    def _():
        attend(n_tiles - 2, True)

    attend(n_tiles - 1, True)

    inv = 1.0 / l_ref[...]
    for h in range(H_A):
        hs = slice(h * BLOCK, (h + 1) * BLOCK)
        o_lat = (acc_ref[:, hs] * inv[:, hs]).T.astype(bf16)
        o_ref[0, :, h * HD_A:(h + 1) * HD_A] = jnp.dot(
            o_lat, wuv_ref[:, h * HD_A:(h + 1) * HD_A],
            preferred_element_type=f32).astype(bf16)


def _dsa(qa, qi, wi, kk, ckv, ckvt, wuk, wuv, bias, tk):
    b, s, _ = qa.shape
    nb = s // BLOCK
    k_sel = min(TOPK_MAX, s // 4)
    idx_bits = max(1, int(math.ceil(math.log2(s))))
    blk = lambda width: pl.BlockSpec((1, BLOCK, width), lambda bi, i: (bi, i, 0))
    full = lambda width: pl.BlockSpec((1, s, width), lambda bi, i: (bi, 0, 0))
    const = lambda bi, i: (0, 0)
    return pl.pallas_call(
        functools.partial(_dsa_kernel, k_sel=k_sel, tk=tk, idx_bits=idx_bits),
        grid=(b, nb),
        in_specs=[
            blk(W_QA), blk(W_QI), blk(LANES),
            full(2 * IDX_DIM), full(KV_RANK),
            pl.BlockSpec((1, s // tk, KV_RANK, tk), lambda bi, i: (bi, 0, 0, 0)),
            pl.BlockSpec((KV_RANK, W_QA), const),
            pl.BlockSpec((KV_RANK, W_QA), const),
            pl.BlockSpec(bias.shape, const),
        ],
        out_specs=blk(W_QA),
        out_shape=jax.ShapeDtypeStruct((b, s, W_QA), jnp.bfloat16),
        scratch_shapes=[
            pltpu.VMEM((s, LANES), jnp.float32),
            pltpu.VMEM((KV_RANK, H_A * BLOCK), jnp.float32),
            pltpu.VMEM((1, H_A * BLOCK), jnp.float32),
            pltpu.VMEM((1, H_A * BLOCK), jnp.float32),
            pltpu.VMEM((KV_RANK, H_A * BLOCK), jnp.bfloat16),
        ],
        compiler_params=pltpu.CompilerParams(
            dimension_semantics=("parallel", "arbitrary"),
            vmem_limit_bytes=VMEM_LIMIT_BYTES),
        name="dsa",
    )(qa, qi, wi, kk, ckv, ckvt, wuk, wuv, bias)


def _swa_kernel(sink_ref, q_ref, kp_ref, kc_ref, vp_ref, vc_ref, bias_ref, o_ref):
    i = pl.program_id(1)
    f32 = jnp.float32
    bf16 = jnp.bfloat16
    lane = lax.broadcasted_iota(jnp.int32, (BLOCK, LANES), 1)
    col = lax.broadcasted_iota(jnp.int32, (BLOCK, 2 * BLOCK), 1)
    prev_ok = (col >= BLOCK) | (i > 0)
    k_all = jnp.concatenate([kp_ref[0], kc_ref[0]], axis=0)
    v_all = jnp.concatenate([vp_ref[0], vc_ref[0]], axis=0)
    for pair in range(H_B // 2):
        kvh = (2 * pair) // GQA_GROUP
        kd = k_all[:, kvh * LANES:(kvh + 1) * LANES]
        vd = v_all[:, kvh * LANES:(kvh + 1) * LANES]
        q2 = q_ref[0, :, pair * LANES:(pair + 1) * LANES].astype(f32)
        halves = []
        for sub in range(2):
            h = 2 * pair + sub
            keep = (lane < HD_B) if sub == 0 else (lane >= HD_B)
            qm = jnp.where(keep, q2, 0.0).astype(bf16)
            lg = lax.dot_general(qm, kd, (((1,), (1,)), ((), ())),
                                 preferred_element_type=f32) + bias_ref[h]
            lg = jnp.where(prev_ok, lg, NEG)
            sink = sink_ref[h]
            m = jnp.maximum(jnp.max(lg, axis=-1, keepdims=True), sink)
            p = jnp.exp(lg - m)
            denom = jnp.sum(p, axis=-1, keepdims=True) + jnp.exp(sink - m)
            halves.append(jnp.dot(p.astype(bf16), vd, preferred_element_type=f32) / denom)
        o_ref[0, :, pair * LANES:(pair + 1) * LANES] = jnp.where(
            lane < HD_B, halves[0], halves[1]).astype(bf16)


def _swa(sinks, qb, kb, vb, bias):
    b, s, _ = qb.shape
    nb = s // BLOCK
    cur = lambda width: pl.BlockSpec((1, BLOCK, width), lambda bi, i: (bi, i, 0))
    prev = lambda width: pl.BlockSpec((1, BLOCK, width), lambda bi, i: (bi, jnp.maximum(i - 1, 0), 0))
    return pl.pallas_call(
        _swa_kernel,
        grid=(b, nb),
        in_specs=[
            pl.BlockSpec(memory_space=pltpu.SMEM),
            cur(W_QB), prev(2 * W_KVB), cur(2 * W_KVB), prev(2 * W_KVB), cur(2 * W_KVB),
            pl.BlockSpec(bias.shape, lambda bi, i: (0, 0, 0)),
        ],
        out_specs=cur(W_QB),
        out_shape=jax.ShapeDtypeStruct((b, s, W_QB), jnp.bfloat16),
        compiler_params=pltpu.CompilerParams(
            dimension_semantics=("parallel", "parallel"),
            vmem_limit_bytes=VMEM_LIMIT_BYTES),
        name="swa",
    )(sinks, qb, kb, kb, vb, vb, bias)


def _out_kernel(h_ref, oa_ref, ob_ref, wa_ref, wb_ref, o_ref):
    o_ref[...] = (h_ref[...]
                  + jnp.dot(oa_ref[...], wa_ref[...], preferred_element_type=jnp.float32)
                  + jnp.dot(ob_ref[...], wb_ref[...], preferred_element_type=jnp.float32))


def _out_proj(h, oa, ob, wa, wb):
    n, d = h.shape
    tm = min(OUT_TOKEN_TILE, n)
    const = lambda i: (0, 0)
    return pl.pallas_call(
        _out_kernel,
        grid=(n // tm,),
        in_specs=[
            pl.BlockSpec((tm, d), lambda i: (i, 0)),
            pl.BlockSpec((tm, oa.shape[1]), lambda i: (i, 0)),
            pl.BlockSpec((tm, ob.shape[1]), lambda i: (i, 0)),
            pl.BlockSpec(wa.shape, const, pipeline_mode=pl.Buffered(1)),
            pl.BlockSpec(wb.shape, const, pipeline_mode=pl.Buffered(1)),
        ],
        out_specs=pl.BlockSpec((tm, d), lambda i: (i, 0)),
        out_shape=jax.ShapeDtypeStruct((n, d), jnp.float32),
        compiler_params=pltpu.CompilerParams(
            dimension_semantics=("parallel",),
            vmem_limit_bytes=VMEM_LIMIT_BYTES),
        name="out_proj",
    )(h, oa, ob, wa, wb)


def _t5_bucket_np(dist):
    n = np.maximum(dist, 0)
    max_exact = N_BUCKETS // 2
    nf = np.maximum(n, 1).astype(np.float32)
    large = max_exact + (np.log(nf / np.float32(max_exact)) / np.float32(math.log(MAX_DISTANCE / max_exact))
                         * np.float32(N_BUCKETS - max_exact)).astype(np.int32)
    large = np.minimum(large, N_BUCKETS - 1)
    return np.where(n < max_exact, n, large)


def _band_distance():
    q = np.arange(BLOCK)[:, None]
    kj = np.arange(2 * BLOCK)[None, :]
    return BLOCK + q - kj


def _dsa_bias(bias_a, tk):
    dist = _band_distance().T
    bucket = jnp.asarray(_t5_bucket_np(dist))
    rel = bias_a[bucket] - bias_a[N_BUCKETS - 1][None, None, :]
    rel = jnp.where(jnp.asarray(dist >= 0)[:, :, None], rel, 0.0)
    rel = jnp.transpose(rel, (0, 2, 1)).reshape(2 * BLOCK, H_A * BLOCK)
    pad = jnp.zeros((tk, H_A * BLOCK), jnp.float32)
    return jnp.concatenate([pad, rel, pad], axis=0)


def _swa_bias(bias_b):
    dist = _band_distance()
    in_window = (dist >= 0) & (dist < WINDOW)
    bias = jnp.moveaxis(bias_b[jnp.asarray(_t5_bucket_np(dist))], -1, 0)
    return jnp.where(jnp.asarray(in_window)[None], bias, NEG)


def _layout_w_in(w):
    splits = np.cumsum([W_QA, KV_RANK, W_QI, IDX_DIM, IDX_HEADS, W_QB, W_KVB])
    qa, ckv, qi, ki, wi, qb, kb, vb = jnp.split(w, [int(c) for c in splits], axis=-1)
    d = w.shape[0]
    dup = lambda a: jnp.concatenate(
        [a[:, kv * HD_B:(kv + 1) * HD_B] for kv in range(KVH_B) for _ in range(2)], axis=-1)
    wi_pad = jnp.concatenate([wi, jnp.zeros((d, LANES - IDX_HEADS), w.dtype)], axis=-1)
    out = jnp.concatenate([qa, ckv, qi, ki, ki, wi_pad, qb, dup(kb), dup(vb)], axis=-1)
    assert out.shape[1] == D_IN_PAD
    return out.astype(jnp.bfloat16)


def kernel(x, rel_bias, ffn1_norm, ffn1_gate, ffn1_up, ffn1_down, mix_norm, w_in, kv_norm,
           idx_k_norm_g, idx_k_norm_b, w_uk, w_uv, sinks, w_out, ffn2_norm, ffn2_gate,
           ffn2_up, ffn2_down, final_norm):
    b, s, d = x.shape
    depth = w_in.shape[0]
    bf = jnp.bfloat16
    tk = min(DSA_KEY_TILE, s)
    assert s % BLOCK == 0 and s % tk == 0

    dsa_bias = _dsa_bias(rel_bias[:, :H_A], tk)
    swa_bias = _swa_bias(rel_bias[:, H_A:])
    fin = final_norm.reshape(1, d)
    dup2 = lambda a: jnp.concatenate([a, a], axis=-1).reshape(1, 2 * IDX_DIM)

    h = x.reshape(b * s, d)
    for l in range(depth):
        h = _ffn(h, ffn1_norm[l].reshape(1, d), ffn1_gate[l].astype(bf), ffn1_up[l].astype(bf),
                 ffn1_down[l].astype(bf), fin, False)
        qa, ckv, ckvt, qi, kk, wi, qb, kb, vb = _proj(
            h.reshape(b, s, d), mix_norm[l].reshape(1, d), _layout_w_in(w_in[l]),
            kv_norm[l].reshape(1, KV_RANK), dup2(idx_k_norm_g[l]), dup2(idx_k_norm_b[l]), tk)
        o_a = _dsa(qa, qi, wi, kk, ckv, ckvt,
                   w_uk[l].reshape(KV_RANK, W_QA).astype(bf),
                   w_uv[l].reshape(KV_RANK, W_QA).astype(bf), dsa_bias, tk)
        o_b = _swa(sinks[l], qb, kb, vb, swa_bias)
        wo = w_out[l].astype(bf)
        h = _out_proj(h, o_a.reshape(b * s, W_QA), o_b.reshape(b * s, W_QB), wo[:W_QA], wo[W_QA:])
        h = _ffn(h, ffn2_norm[l].reshape(1, d), ffn2_gate[l].astype(bf), ffn2_up[l].astype(bf),
                 ffn2_down[l].astype(bf), fin, l == depth - 1)
    return h.reshape(b, s, d)
```

```python
import functools
import math

import numpy as np
import jax
import jax.numpy as jnp
from jax import lax
from jax.experimental import pallas as pl
from jax.experimental.pallas import tpu as pltpu

H_A = 8
HD_A = 128
KV_RANK = 256
IDX_HEADS = 8
IDX_DIM = 64
TOPK_MAX = 256
H_B = 16
KVH_B = 2
GQA_GROUP = H_B // KVH_B
HD_B = 64
WINDOW = 128
BLOCK = 128
N_BUCKETS = 32
MAX_DISTANCE = 128
EPS = 1e-6

LANES = 128
SUBLANES = 8
VMEM_LIMIT_BYTES = 56 * 1024 * 1024

FFN_TOKEN_TILE = 1024
FFN_HIDDEN_TILE = 512
PROJ_TOKEN_TILE = 512
OUT_TOKEN_TILE = 512
DSA_KEY_TILE = 512
COUNT_ACCUMULATORS = 4
ATTN_HEADS_PER_CHAIN = 2

NEG = -1e30
F32_LOWEST = float(np.finfo(np.float32).min)
LOG2E = math.log2(math.e)

W_QA = H_A * HD_A
W_QI = IDX_HEADS * IDX_DIM
W_QB = H_B * HD_B
W_KVB = KVH_B * HD_B
OFF_QA = 0
OFF_CKV = OFF_QA + W_QA
OFF_QI = OFF_CKV + KV_RANK
OFF_KK = OFF_QI + W_QI
OFF_WI = OFF_KK + 2 * IDX_DIM
OFF_QB = OFF_WI + LANES
OFF_KB = OFF_QB + W_QB
OFF_VB = OFF_KB + 2 * W_KVB
D_IN_PAD = OFF_VB + 2 * W_KVB


def _rms(x, g):
    return x * lax.rsqrt(jnp.mean(x * x, axis=-1, keepdims=True) + EPS) * g


def _ffn_kernel(h_ref, g_ref, wg_ref, wu_ref, wd_ref, fg_ref, o_ref, xn_ref, *, final_norm):
    j = pl.program_id(1)

    @pl.when(j == 0)
    def _():
        x = h_ref[...]
        xn_ref[...] = _rms(x, g_ref[...]).astype(jnp.bfloat16)
        o_ref[...] = x

    xn = xn_ref[...]
    gate = jnp.dot(xn, wg_ref[...], preferred_element_type=jnp.float32)
    up = jnp.dot(xn, wu_ref[...], preferred_element_type=jnp.float32)
    act = (gate * jax.nn.sigmoid(gate) * (0.5 * up)).astype(jnp.bfloat16)
    o_ref[...] += jnp.dot(act, wd_ref[...], preferred_element_type=jnp.float32)

    if final_norm:
        @pl.when(j == pl.num_programs(1) - 1)
        def _():
            o_ref[...] = _rms(o_ref[...], fg_ref[...])


def _ffn(h, g, wg, wu, wd, fg, final_norm):
    n, d = h.shape
    f = wg.shape[1]
    tm = min(FFN_TOKEN_TILE, n)
    tf = FFN_HIDDEN_TILE
    return pl.pallas_call(
        functools.partial(_ffn_kernel, final_norm=final_norm),
        grid=(n // tm, f // tf),
        in_specs=[
            pl.BlockSpec((tm, d), lambda i, j: (i, 0), pipeline_mode=pl.Buffered(1)),
            pl.BlockSpec((1, d), lambda i, j: (0, 0)),
            pl.BlockSpec((d, tf), lambda i, j: (0, j)),
            pl.BlockSpec((d, tf), lambda i, j: (0, j)),
            pl.BlockSpec((tf, d), lambda i, j: (j, 0)),
            pl.BlockSpec((1, d), lambda i, j: (0, 0)),
        ],
        out_specs=pl.BlockSpec((tm, d), lambda i, j: (i, 0)),
        out_shape=jax.ShapeDtypeStruct((n, d), jnp.float32),
        scratch_shapes=[pltpu.VMEM((tm, d), jnp.bfloat16)],
        compiler_params=pltpu.CompilerParams(
            dimension_semantics=("parallel", "arbitrary"),
            vmem_limit_bytes=VMEM_LIMIT_BYTES),
        name="ffn",
    )(h, g, wg, wu, wd, fg)


def _proj_kernel(h_ref, g_ref, w_ref, kvg_ref, kg_ref, kb_ref,
                 qa_o, ckv_o, ckvt_o, qi_o, kk_o, wi_o, qb_o, kb_o, vb_o, xn_ref):
    xn_ref[...] = _rms(h_ref[0], g_ref[...]).astype(jnp.bfloat16)

    def seg(lo, width):
        return jnp.dot(xn_ref[...], w_ref[:, lo:lo + width], preferred_element_type=jnp.float32)

    qa_o[0] = seg(OFF_QA, W_QA).astype(jnp.bfloat16)

    c = _rms(seg(OFF_CKV, KV_RANK), kvg_ref[...])
    ckv_o[0] = c.astype(jnp.bfloat16)
    ct = c.T
    for t in range(ckvt_o.shape[1]):
        ckvt_o[0, t] = ct[:, t * BLOCK:(t + 1) * BLOCK].astype(jnp.bfloat16)

    qi_o[0] = seg(OFF_QI, W_QI).astype(jnp.bfloat16)

    k = seg(OFF_KK, 2 * IDX_DIM)
    mu = jnp.mean(k, axis=-1, keepdims=True)
    kc = k - mu
    var = jnp.mean(kc * kc, axis=-1, keepdims=True)
    kk_o[0] = (kc * lax.rsqrt(var + EPS) * kg_ref[...] + kb_ref[...]).astype(jnp.bfloat16)

    wi_o[0] = seg(OFF_WI, LANES) * (IDX_HEADS ** -0.5 * IDX_DIM ** -0.5)
    qb_o[0] = (seg(OFF_QB, W_QB) * (HD_B ** -0.5)).astype(jnp.bfloat16)
    kb_o[0] = seg(OFF_KB, 2 * W_KVB).astype(jnp.bfloat16)
    vb_o[0] = seg(OFF_VB, 2 * W_KVB).astype(jnp.bfloat16)


def _proj(h, g, w, kvg, kg, kb):
    b, s, d = h.shape
    tm = min(PROJ_TOKEN_TILE, s)
    bf = jnp.bfloat16

    def tok(width, dtype):
        return (jax.ShapeDtypeStruct((b, s, width), dtype),
                pl.BlockSpec((1, tm, width), lambda bi, ti: (bi, ti, 0)))

    outs = [
        tok(W_QA, bf),
        tok(KV_RANK, bf),
        (jax.ShapeDtypeStruct((b, s // BLOCK, KV_RANK, BLOCK), bf),
         pl.BlockSpec((1, tm // BLOCK, KV_RANK, BLOCK), lambda bi, ti: (bi, ti, 0, 0))),
        tok(W_QI, bf),
        tok(2 * IDX_DIM, bf),
        tok(LANES, jnp.float32),
        tok(W_QB, bf),
        tok(2 * W_KVB, bf),
        tok(2 * W_KVB, bf),
    ]
    const = lambda bi, ti: (0, 0)
    return pl.pallas_call(
        _proj_kernel,
        grid=(b, s // tm),
        in_specs=[
            pl.BlockSpec((1, tm, d), lambda bi, ti: (bi, ti, 0)),
            pl.BlockSpec((1, d), const),
            pl.BlockSpec((d, D_IN_PAD), const, pipeline_mode=pl.Buffered(1)),
            pl.BlockSpec((1, KV_RANK), const),
            pl.BlockSpec((1, 2 * IDX_DIM), const),
            pl.BlockSpec((1, 2 * IDX_DIM), const),
        ],
        out_specs=[o[1] for o in outs],
        out_shape=[o[0] for o in outs],
        scratch_shapes=[pltpu.VMEM((tm, d), bf)],
        compiler_params=pltpu.CompilerParams(
            dimension_semantics=("parallel", "parallel"),
            vmem_limit_bytes=VMEM_LIMIT_BYTES),
        name="proj",
    )(h, g, w, kvg, kg, kb)


def _sortable_to_f32(u):
    k = u ^ jnp.int32(-2 ** 31)
    bits = k ^ ((k >> 31) & jnp.int32(0x7FFFFFFF))
    return lax.bitcast_convert_type(bits, jnp.float32)


def _dsa_kernel(qa_ref, qi_ref, wi_ref, kk_ref, ckv_ref, ckvt_ref, wuk_ref, wuv_ref, bias_ref,
                o_ref, score_ref, acc_ref, m_ref, l_ref, qlat_ref,
                x_a, x_b, p_a, p_b, alpha_a, alpha_b, *, k_sel, tk, idx_bits):
    n_chains = H_A // ATTN_HEADS_PER_CHAIN
    chain_cols = ATTN_HEADS_PER_CHAIN * BLOCK
    i = pl.program_id(1)
    blocks_per_tile = tk // BLOCK
    n_tiles = i // blocks_per_tile + 1
    f32 = jnp.float32
    bf16 = jnp.bfloat16

    lane = lax.broadcasted_iota(jnp.int32, (BLOCK, LANES), 1)
    parts = []
    for h in range(IDX_HEADS):
        pair = qi_ref[0, :, (h // 2) * LANES:(h // 2 + 1) * LANES].astype(f32)
        keep = (lane < IDX_DIM) if h % 2 == 0 else (lane >= IDX_DIM)
        parts.append(jnp.where(keep, pair, 0.0).astype(bf16))
    qm = jnp.concatenate(parts, axis=0)
    wit = wi_ref[0].T

    t_pos = i * BLOCK + lax.broadcasted_iota(jnp.int32, (tk, LANES), 1)
    row = lax.broadcasted_iota(jnp.int32, (tk, LANES), 0)

    def score_body(kt, carry):
        base = pl.multiple_of(kt * tk, tk)
        d = lax.dot_general(kk_ref[0, pl.ds(base, tk), :], qm, (((1,), (1,)), ((), ())),
                            preferred_element_type=f32)
        sc = jnp.zeros((tk, LANES), f32)
        for h in range(IDX_HEADS):
            sc = sc + jnp.maximum(d[:, h * BLOCK:(h + 1) * BLOCK], 0.0) * wit[h:h + 1, :]
        score_ref[pl.ds(base, tk), :] = jnp.where(base + row <= t_pos, sc, -jnp.inf)
        return carry

    lax.fori_loop(0, n_tiles, score_body, 0)

    acc_rows = COUNT_ACCUMULATORS * SUBLANES

    def count(pred):
        def body(c, acc):
            base = pl.multiple_of(c * tk, tk)
            hit = jnp.where(pred(score_ref[pl.ds(base, tk), :], base), 1, 0)
            return acc + hit.reshape(tk // acc_rows, acc_rows, LANES).sum(axis=0)
        acc = lax.fori_loop(0, n_tiles, body, jnp.zeros((acc_rows, LANES), jnp.int32))
        return jnp.sum(acc, axis=0, keepdims=True)

    def bit_body(it, u):
        cand = u | lax.shift_left(jnp.int32(1), 31 - it)
        thr_c = _sortable_to_f32(cand)
        cnt = count(lambda blk, base: blk >= thr_c)
        return jnp.where(cnt >= k_sel, cand, u)

    u = lax.fori_loop(0, 32, bit_body, jnp.zeros((1, LANES), jnp.int32))
    thr = _sortable_to_f32(u)
    thr = jnp.where(thr >= F32_LOWEST, thr, F32_LOWEST)

    cnt_ge = count(lambda blk, base: blk >= thr)
    has_tie = jnp.max(jnp.where(cnt_ge > k_sel, 1.0, 0.0)) > 0.5

    @pl.when(has_tie)
    def _():
        need = k_sel - count(lambda blk, base: blk > thr)

        def idx_body(it, p):
            cand = p | lax.shift_left(jnp.int32(1), idx_bits - 1 - it)
            cnt = count(lambda blk, base: (blk == thr) & (base + row < cand))
            return jnp.where(cnt < need, cand, p)

        last = lax.fori_loop(0, idx_bits, idx_body, jnp.zeros((1, LANES), jnp.int32))

        def drop_body(c, carry):
            base = pl.multiple_of(c * tk, tk)
            blk = score_ref[pl.ds(base, tk), :]
            score_ref[pl.ds(base, tk), :] = jnp.where(
                (blk == thr) & (base + row > last), -jnp.inf, blk)
            return carry

        lax.fori_loop(0, n_tiles, drop_body, 0)

    def mask_body(c, carry):
        rows = pl.ds(pl.multiple_of(c * tk, tk), tk)
        score_ref[rows, :] = jnp.where(score_ref[rows, :] >= thr, 0.0, NEG)
        return carry

    lax.fori_loop(0, n_tiles, mask_body, 0)

    scale = HD_A ** -0.5 * LOG2E
    for h in range(H_A):
        hs = slice(h * HD_A, (h + 1) * HD_A)
        qlt = lax.dot_general(wuk_ref[:, hs], qa_ref[0, :, hs], (((1,), (1,)), ((), ())),
                              preferred_element_type=f32)
        g, a = divmod(h, ATTN_HEADS_PER_CHAIN)
        qlat_ref[g, :, a * BLOCK:(a + 1) * BLOCK] = (qlt * scale).astype(bf16)

    def tile_mask(rows):
        return jnp.concatenate([score_ref[rows, :]] * ATTN_HEADS_PER_CHAIN, axis=1)

    near_blk = jnp.maximum(i - 1, 0)
    near_base = pl.multiple_of(near_blk * BLOCK, BLOCK)
    bias_off = pl.multiple_of(jnp.where(i > 0, 0, BLOCK), BLOCK)
    near_rows = pl.ds(near_base, 2 * BLOCK)
    near_c = ckv_ref[0, near_rows, :]
    near_ct = jnp.concatenate([ckvt_ref[0, near_blk], ckvt_ref[0, near_blk + 1]], axis=1)
    near_mask = tile_mask(near_rows)
    for g in range(n_chains):
        x = (jnp.dot(near_c, qlat_ref[g], preferred_element_type=f32)
             + bias_ref[pl.ds(bias_off, 2 * BLOCK), g * chain_cols:(g + 1) * chain_cols] + near_mask)
        m_new = jnp.max(x, axis=0, keepdims=True)
        p = jnp.exp2(x - m_new)
        m_ref[g] = m_new
        l_ref[g] = jnp.sum(p, axis=0, keepdims=True)
        acc_ref[g] = jnp.dot(near_ct, p.astype(bf16), preferred_element_type=f32)
    score_ref[near_rows, :] = jnp.full((2 * BLOCK, LANES), NEG, f32)

    n_far = (i - 1 + blocks_per_tile - 1) // blocks_per_tile
    n_items = n_far * n_chains

    def logits_stage(n, x_ref):
        kt = jnp.minimum(n // n_chains, n_far - 1)
        rows = pl.ds(pl.multiple_of(kt * tk, tk), tk)
        x_ref[...] = jnp.dot(ckv_ref[0, rows, :], qlat_ref[n % n_chains],
                             preferred_element_type=f32) + tile_mask(rows)

    def softmax_stage(n, x_ref, p_ref, alpha_ref):
        g = n % n_chains
        x = x_ref[...]
        m_old = m_ref[g]
        m_new = jnp.maximum(m_old, jnp.max(x, axis=0, keepdims=True))
        p = jnp.exp2(x - m_new)
        alpha = jnp.exp2(m_old - m_new)
        l_ref[g] = alpha * l_ref[g] + jnp.sum(p, axis=0, keepdims=True)
        m_ref[g] = m_new
        alpha_ref[...] = alpha
        p_ref[...] = p.astype(bf16)

    def values_stage(n, p_ref, alpha_ref):
        kt = jnp.maximum(n, 0) // n_chains
        g = (n + n_chains) % n_chains
        c_t = jnp.concatenate(
            [ckvt_ref[0, kt * blocks_per_tile + j] for j in range(blocks_per_tile)], axis=1)
        acc_ref[g] = alpha_ref[...] * acc_ref[g] + jnp.dot(
            c_t, p_ref[...], preferred_element_type=f32)

    @pl.when(n_far > 0)
    def _():
        p_b[...] = jnp.zeros(p_b.shape, bf16)
        alpha_b[...] = jnp.ones(alpha_b.shape, f32)
        logits_stage(0, x_a)

        def pair_body(j, carry):
            n = 2 * j
            logits_stage(n + 1, x_b)
            softmax_stage(n, x_a, p_a, alpha_a)
            values_stage(n - 1, p_b, alpha_b)
            logits_stage(n + 2, x_a)
            softmax_stage(n + 1, x_b, p_b, alpha_b)
            values_stage(n, p_a, alpha_a)
            return carry

        lax.fori_loop(0, n_items // 2, pair_body, 0)
        values_stage(n_items - 1, p_b, alpha_b)

    for h in range(H_A):
        g, a = divmod(h, ATTN_HEADS_PER_CHAIN)
        hs = slice(a * BLOCK, (a + 1) * BLOCK)
        o_lat = (acc_ref[g, :, hs] / l_ref[g, :, hs]).T.astype(bf16)
        o_ref[0, :, h * HD_A:(h + 1) * HD_A] = jnp.dot(
            o_lat, wuv_ref[:, h * HD_A:(h + 1) * HD_A],
            preferred_element_type=f32).astype(bf16)


def _dsa(qa, qi, wi, kk, ckv, ckvt, wuk, wuv, bias, tk):
    b, s, _ = qa.shape
    nb = s // BLOCK
    assert nb >= 2
    k_sel = min(TOPK_MAX, s // 4)
    idx_bits = max(1, int(math.ceil(math.log2(s))))
    cols = H_A * BLOCK
    chain_cols = ATTN_HEADS_PER_CHAIN * BLOCK
    n_chains = H_A // ATTN_HEADS_PER_CHAIN
    blk = lambda width: pl.BlockSpec((1, BLOCK, width), lambda bi, i: (bi, i, 0))
    full = lambda width: pl.BlockSpec((1, s, width), lambda bi, i: (bi, 0, 0))
    const = lambda bi, i: (0, 0)
    return pl.pallas_call(
        functools.partial(_dsa_kernel, k_sel=k_sel, tk=tk, idx_bits=idx_bits),
        grid=(b, nb),
        in_specs=[
            blk(W_QA), blk(W_QI), blk(LANES),
            full(2 * IDX_DIM), full(KV_RANK),
            pl.BlockSpec((1, nb, KV_RANK, BLOCK), lambda bi, i: (bi, 0, 0, 0)),
            pl.BlockSpec((KV_RANK, W_QA), const),
            pl.BlockSpec((KV_RANK, W_QA), const),
            pl.BlockSpec(bias.shape, const),
        ],
        out_specs=blk(W_QA),
        out_shape=jax.ShapeDtypeStruct((b, s, W_QA), jnp.bfloat16),
        scratch_shapes=[
            pltpu.VMEM((s, LANES), jnp.float32),
            pltpu.VMEM((n_chains, KV_RANK, chain_cols), jnp.float32),
            pltpu.VMEM((n_chains, 1, chain_cols), jnp.float32),
            pltpu.VMEM((n_chains, 1, chain_cols), jnp.float32),
            pltpu.VMEM((n_chains, KV_RANK, chain_cols), jnp.bfloat16),
            pltpu.VMEM((tk, chain_cols), jnp.float32),
            pltpu.VMEM((tk, chain_cols), jnp.float32),
            pltpu.VMEM((tk, chain_cols), jnp.bfloat16),
            pltpu.VMEM((tk, chain_cols), jnp.bfloat16),
            pltpu.VMEM((1, chain_cols), jnp.float32),
            pltpu.VMEM((1, chain_cols), jnp.float32),
        ],
        compiler_params=pltpu.CompilerParams(
            dimension_semantics=("parallel", "arbitrary"),
            vmem_limit_bytes=VMEM_LIMIT_BYTES),
        name="dsa",
    )(qa, qi, wi, kk, ckv, ckvt, wuk, wuv, bias)


def _swa_kernel(sink_ref, q_ref, kp_ref, kc_ref, vp_ref, vc_ref, bias_ref, o_ref):
    i = pl.program_id(1)
    f32 = jnp.float32
    bf16 = jnp.bfloat16
    lane = lax.broadcasted_iota(jnp.int32, (BLOCK, LANES), 1)
    col = lax.broadcasted_iota(jnp.int32, (BLOCK, 2 * BLOCK), 1)
    prev_ok = (col >= BLOCK) | (i > 0)
    k_all = jnp.concatenate([kp_ref[0], kc_ref[0]], axis=0)
    v_all = jnp.concatenate([vp_ref[0], vc_ref[0]], axis=0)
    for pair in range(H_B // 2):
        kvh = (2 * pair) // GQA_GROUP
        kd = k_all[:, kvh * LANES:(kvh + 1) * LANES]
        vd = v_all[:, kvh * LANES:(kvh + 1) * LANES]
        q2 = q_ref[0, :, pair * LANES:(pair + 1) * LANES].astype(f32)
        halves = []
        for sub in range(2):
            h = 2 * pair + sub
            keep = (lane < HD_B) if sub == 0 else (lane >= HD_B)
            qm = jnp.where(keep, q2, 0.0).astype(bf16)
            lg = lax.dot_general(qm, kd, (((1,), (1,)), ((), ())),
                                 preferred_element_type=f32) + bias_ref[h]
            lg = jnp.where(prev_ok, lg, NEG)
            sink = sink_ref[h]
            m = jnp.maximum(jnp.max(lg, axis=-1, keepdims=True), sink)
            p = jnp.exp(lg - m)
            denom = jnp.sum(p, axis=-1, keepdims=True) + jnp.exp(sink - m)
            halves.append(jnp.dot(p.astype(bf16), vd, preferred_element_type=f32) / denom)
        o_ref[0, :, pair * LANES:(pair + 1) * LANES] = jnp.where(
            lane < HD_B, halves[0], halves[1]).astype(bf16)


def _swa(sinks, qb, kb, vb, bias):
    b, s, _ = qb.shape
    nb = s // BLOCK
    cur = lambda width: pl.BlockSpec((1, BLOCK, width), lambda bi, i: (bi, i, 0))
    prev = lambda width: pl.BlockSpec((1, BLOCK, width), lambda bi, i: (bi, jnp.maximum(i - 1, 0), 0))
    return pl.pallas_call(
        _swa_kernel,
        grid=(b, nb),
        in_specs=[
            pl.BlockSpec(memory_space=pltpu.SMEM),
            cur(W_QB), prev(2 * W_KVB), cur(2 * W_KVB), prev(2 * W_KVB), cur(2 * W_KVB),
            pl.BlockSpec(bias.shape, lambda bi, i: (0, 0, 0)),
        ],
        out_specs=cur(W_QB),
        out_shape=jax.ShapeDtypeStruct((b, s, W_QB), jnp.bfloat16),
        compiler_params=pltpu.CompilerParams(
            dimension_semantics=("parallel", "parallel"),
            vmem_limit_bytes=VMEM_LIMIT_BYTES),
        name="swa",
    )(sinks, qb, kb, kb, vb, vb, bias)


def _out_kernel(h_ref, oa_ref, ob_ref, wa_ref, wb_ref, o_ref):
    o_ref[...] = (h_ref[...]
                  + jnp.dot(oa_ref[...], wa_ref[...], preferred_element_type=jnp.float32)
                  + jnp.dot(ob_ref[...], wb_ref[...], preferred_element_type=jnp.float32))


def _out_proj(h, oa, ob, w):
    n, d = h.shape
    tm = min(OUT_TOKEN_TILE, n)
    wa_rows, wb_rows = oa.shape[1], ob.shape[1]
    assert wa_rows == wb_rows and w.shape[0] == wa_rows + wb_rows
    return pl.pallas_call(
        _out_kernel,
        grid=(n // tm,),
        in_specs=[
            pl.BlockSpec((tm, d), lambda i: (i, 0)),
            pl.BlockSpec((tm, wa_rows), lambda i: (i, 0)),
            pl.BlockSpec((tm, wb_rows), lambda i: (i, 0)),
            pl.BlockSpec((wa_rows, d), lambda i: (0, 0), pipeline_mode=pl.Buffered(1)),
            pl.BlockSpec((wb_rows, d), lambda i: (1, 0), pipeline_mode=pl.Buffered(1)),
        ],
        out_specs=pl.BlockSpec((tm, d), lambda i: (i, 0)),
        out_shape=jax.ShapeDtypeStruct((n, d), jnp.float32),
        compiler_params=pltpu.CompilerParams(
            dimension_semantics=("parallel",),
            vmem_limit_bytes=VMEM_LIMIT_BYTES),
        name="out_proj",
    )(h, oa, ob, w, w)


def _t5_bucket_np(dist):
    n = np.maximum(dist, 0)
    max_exact = N_BUCKETS // 2
    nf = np.maximum(n, 1).astype(np.float32)
    large = max_exact + (np.log(nf / np.float32(max_exact)) / np.float32(math.log(MAX_DISTANCE / max_exact))
                         * np.float32(N_BUCKETS - max_exact)).astype(np.int32)
    large = np.minimum(large, N_BUCKETS - 1)
    return np.where(n < max_exact, n, large)


def _band_distance():
    q = np.arange(BLOCK)[:, None]
    kj = np.arange(2 * BLOCK)[None, :]
    return BLOCK + q - kj


def _band_bias_kernel(table_ref, bucket_ref, o_ref):
    h = pl.program_id(0)
    bucket = bucket_ref[...]
    out = jnp.zeros(bucket.shape, jnp.float32)
    for b in range(N_BUCKETS):
        out = jnp.where(bucket == b, table_ref[b, h], out)
    o_ref[0] = out


def _band_bias(rel_bias):
    n_heads = rel_bias.shape[1]
    bucket = jnp.asarray(_t5_bucket_np(_band_distance()).astype(np.int32))
    return pl.pallas_call(
        _band_bias_kernel,
        grid=(n_heads,),
        in_specs=[
            pl.BlockSpec(memory_space=pltpu.SMEM),
            pl.BlockSpec(bucket.shape, lambda h: (0, 0)),
        ],
        out_specs=pl.BlockSpec((1,) + bucket.shape, lambda h: (h, 0, 0)),
        out_shape=jax.ShapeDtypeStruct((n_heads,) + bucket.shape, jnp.float32),
        compiler_params=pltpu.CompilerParams(dimension_semantics=("parallel",)),
        name="band_bias",
    )(rel_bias, bucket)


def _layout_w_in(w):
    splits = np.cumsum([W_QA, KV_RANK, W_QI, IDX_DIM, IDX_HEADS, W_QB, W_KVB])
    qa, ckv, qi, ki, wi, qb, kb, vb = jnp.split(w, [int(c) for c in splits], axis=-1)
    d = w.shape[0]
    dup = lambda a: jnp.concatenate(
        [a[:, kv * HD_B:(kv + 1) * HD_B] for kv in range(KVH_B) for _ in range(2)], axis=-1)
    wi_pad = jnp.concatenate([wi, jnp.zeros((d, LANES - IDX_HEADS), w.dtype)], axis=-1)
    out = jnp.concatenate([qa, ckv, qi, ki, ki, wi_pad, qb, dup(kb), dup(vb)], axis=-1)
    assert out.shape[1] == D_IN_PAD
    return out.astype(jnp.bfloat16)


def kernel(x, rel_bias, ffn1_norm, ffn1_gate, ffn1_up, ffn1_down, mix_norm, w_in, kv_norm,
           idx_k_norm_g, idx_k_norm_b, w_uk, w_uv, sinks, w_out, ffn2_norm, ffn2_gate,
           ffn2_up, ffn2_down, final_norm):
    b, s, d = x.shape
    depth = w_in.shape[0]
    bf = jnp.bfloat16
    tk = min(DSA_KEY_TILE, s)
    assert s % BLOCK == 0 and s % tk == 0

    band = _band_bias(rel_bias)
    dist = _band_distance()
    rel = (band[:H_A] - rel_bias[N_BUCKETS - 1, :H_A][:, None, None]) * LOG2E
    rel = jnp.where(jnp.asarray(dist >= 0)[None], rel, 0.0)
    rel = jnp.transpose(rel, (2, 0, 1)).reshape(2 * BLOCK, H_A * BLOCK)
    dsa_bias = jnp.concatenate([rel, jnp.zeros((BLOCK, H_A * BLOCK), jnp.float32)], axis=0)
    swa_bias = jnp.where(jnp.asarray((dist >= 0) & (dist < WINDOW))[None], band[H_A:], NEG)

    fin = final_norm.reshape(1, d)
    dup2 = lambda a: jnp.concatenate([a, a], axis=-1).reshape(1, 2 * IDX_DIM)

    h = x.reshape(b * s, d)
    for l in range(depth):
        h = _ffn(h, ffn1_norm[l].reshape(1, d), ffn1_gate[l].astype(bf), ffn1_up[l].astype(bf),
                 ffn1_down[l].astype(bf), fin, False)
        qa, ckv, ckvt, qi, kk, wi, qb, kb, vb = _proj(
            h.reshape(b, s, d), mix_norm[l].reshape(1, d), _layout_w_in(w_in[l]),
            kv_norm[l].reshape(1, KV_RANK), dup2(idx_k_norm_g[l]), dup2(idx_k_norm_b[l]))
        o_a = _dsa(qa, qi, wi, kk, ckv, ckvt,
                   w_uk[l].reshape(KV_RANK, W_QA).astype(bf),
                   w_uv[l].reshape(KV_RANK, W_QA).astype(bf), dsa_bias, tk)
        o_b = _swa(sinks[l], qb, kb, vb, swa_bias)
        h = _out_proj(h, o_a.reshape(b * s, W_QA), o_b.reshape(b * s, W_QB), w_out[l].astype(bf))
        h = _ffn(h, ffn2_norm[l].reshape(1, d), ffn2_gate[l].astype(bf), ffn2_up[l].astype(bf),
                 ffn2_down[l].astype(bf), fin, l == depth - 1)
    return h.reshape(b, s, d)
```

```python
import functools
import math

import numpy as np
import jax
import jax.numpy as jnp
from jax import lax
from jax.experimental import pallas as pl
from jax.experimental.pallas import tpu as pltpu

H_A = 8
HD_A = 128
KV_RANK = 256
IDX_HEADS = 8
IDX_DIM = 64
TOPK_MAX = 256
H_B = 16
KVH_B = 2
GQA_GROUP = H_B // KVH_B
HD_B = 64
WINDOW = 128
BLOCK = 128
N_BUCKETS = 32
MAX_DISTANCE = 128
EPS = 1e-6

LANES = 128
SUBLANES = 8
VMEM_LIMIT_BYTES = 56 * 1024 * 1024

FFN_TOKEN_TILE = 1024
FFN_HIDDEN_TILE = 512
PROJ_TOKEN_TILE = 512
OUT_TOKEN_TILE = 512
DSA_KEY_TILE = 512
COUNT_ACCUMULATORS = 4
ATTN_HEADS_PER_CHAIN = 2
SCORE_HEADS_PER_DOT = 2
PIPELINE_LOOKAHEAD = 2

NEG = -1e30
F32_LOWEST = float(np.finfo(np.float32).min)
LOG2E = math.log2(math.e)

W_QA = H_A * HD_A
W_QI = IDX_HEADS * IDX_DIM
W_QB = H_B * HD_B
W_KVB = KVH_B * HD_B
OFF_QA = 0
OFF_CKV = OFF_QA + W_QA
OFF_QI = OFF_CKV + KV_RANK
OFF_KK = OFF_QI + W_QI
OFF_WI = OFF_KK + 2 * IDX_DIM
OFF_QB = OFF_WI + LANES
OFF_KB = OFF_QB + W_QB
OFF_VB = OFF_KB + 2 * W_KVB
D_IN_PAD = OFF_VB + 2 * W_KVB


def _rms(x, g):
    return x * lax.rsqrt(jnp.mean(x * x, axis=-1, keepdims=True) + EPS) * g


def _ffn_kernel(h_ref, g_ref, wg_ref, wu_ref, wd_ref, fg_ref, o_ref, xn_ref, *, final_norm):
    j = pl.program_id(1)

    @pl.when(j == 0)
    def _():
        x = h_ref[...]
        xn_ref[...] = _rms(x, g_ref[...]).astype(jnp.bfloat16)
        o_ref[...] = x

    xn = xn_ref[...]
    gate = jnp.dot(xn, wg_ref[0], preferred_element_type=jnp.float32)
    up = jnp.dot(xn, wu_ref[0], preferred_element_type=jnp.float32)
    act = (gate * jax.nn.sigmoid(gate) * (0.5 * up)).astype(jnp.bfloat16)
    o_ref[...] += jnp.dot(act, wd_ref[...], preferred_element_type=jnp.float32)

    if final_norm:
        @pl.when(j == pl.num_programs(1) - 1)
        def _():
            o_ref[...] = _rms(o_ref[...], fg_ref[...])


def _ffn(h, g, wg, wu, wd, fg, final_norm):
    n, d = h.shape
    tf = FFN_HIDDEN_TILE
    f = wg.shape[0] * tf
    assert wg.shape == wu.shape == (f // tf, d, tf) and wd.shape == (f, d)
    tm = min(FFN_TOKEN_TILE, n)
    return pl.pallas_call(
        functools.partial(_ffn_kernel, final_norm=final_norm),
        grid=(n // tm, f // tf),
        in_specs=[
            pl.BlockSpec((tm, d), lambda i, j: (i, 0)),
            pl.BlockSpec((1, d), lambda i, j: (0, 0)),
            pl.BlockSpec((1, d, tf), lambda i, j: (j, 0, 0)),
            pl.BlockSpec((1, d, tf), lambda i, j: (j, 0, 0)),
            pl.BlockSpec((tf, d), lambda i, j: (j, 0)),
            pl.BlockSpec((1, d), lambda i, j: (0, 0)),
        ],
        out_specs=pl.BlockSpec((tm, d), lambda i, j: (i, 0)),
        out_shape=jax.ShapeDtypeStruct((n, d), jnp.float32),
        scratch_shapes=[pltpu.VMEM((tm, d), jnp.bfloat16)],
        compiler_params=pltpu.CompilerParams(
            dimension_semantics=("parallel", "arbitrary"),
            vmem_limit_bytes=VMEM_LIMIT_BYTES),
        name="ffn",
    )(h, g, wg, wu, wd, fg)


def _proj_kernel(h_ref, g_ref, w_ref, kvg_ref, kg_ref, kb_ref,
                 qa_o, ckv_o, ckvt_o, qi_o, kk_o, wi_o, qb_o, kb_o, vb_o, xn_ref):
    xn_ref[...] = _rms(h_ref[0], g_ref[...]).astype(jnp.bfloat16)

    def seg(lo, width):
        return jnp.dot(xn_ref[...], w_ref[:, lo:lo + width], preferred_element_type=jnp.float32)

    qa_o[0] = seg(OFF_QA, W_QA).astype(jnp.bfloat16)

    c = _rms(seg(OFF_CKV, KV_RANK), kvg_ref[...])
    ckv_o[0] = c.astype(jnp.bfloat16)
    ct = c.T
    for t in range(ckvt_o.shape[1]):
        ckvt_o[0, t] = ct[:, t * BLOCK:(t + 1) * BLOCK].astype(jnp.bfloat16)

    qi_o[0] = seg(OFF_QI, W_QI).astype(jnp.bfloat16)

    k = seg(OFF_KK, 2 * IDX_DIM)
    mu = jnp.mean(k, axis=-1, keepdims=True)
    kc = k - mu
    var = jnp.mean(kc * kc, axis=-1, keepdims=True)
    kk_o[0] = (kc * lax.rsqrt(var + EPS) * kg_ref[...] + kb_ref[...]).astype(jnp.bfloat16)

    wi_o[0] = seg(OFF_WI, LANES) * (IDX_HEADS ** -0.5 * IDX_DIM ** -0.5)
    qb_o[0] = (seg(OFF_QB, W_QB) * (HD_B ** -0.5)).astype(jnp.bfloat16)
    kb_o[0] = seg(OFF_KB, 2 * W_KVB).astype(jnp.bfloat16)
    vb_o[0] = seg(OFF_VB, 2 * W_KVB).astype(jnp.bfloat16)


def _proj(h, g, w, kvg, kg, kb):
    b, s, d = h.shape
    tm = min(PROJ_TOKEN_TILE, s)
    bf = jnp.bfloat16

    def tok(width, dtype):
        return (jax.ShapeDtypeStruct((b, s, width), dtype),
                pl.BlockSpec((1, tm, width), lambda bi, ti: (bi, ti, 0)))

    outs = [
        tok(W_QA, bf),
        tok(KV_RANK, bf),
        (jax.ShapeDtypeStruct((b, s // BLOCK, KV_RANK, BLOCK), bf),
         pl.BlockSpec((1, tm // BLOCK, KV_RANK, BLOCK), lambda bi, ti: (bi, ti, 0, 0))),
        tok(W_QI, bf),
        tok(2 * IDX_DIM, bf),
        tok(LANES, jnp.float32),
        tok(W_QB, bf),
        tok(2 * W_KVB, bf),
        tok(2 * W_KVB, bf),
    ]
    const = lambda bi, ti: (0, 0)
    return pl.pallas_call(
        _proj_kernel,
        grid=(b, s // tm),
        in_specs=[
            pl.BlockSpec((1, tm, d), lambda bi, ti: (bi, ti, 0)),
            pl.BlockSpec((1, d), const),
            pl.BlockSpec((d, D_IN_PAD), const, pipeline_mode=pl.Buffered(1)),
            pl.BlockSpec((1, KV_RANK), const),
            pl.BlockSpec((1, 2 * IDX_DIM), const),
            pl.BlockSpec((1, 2 * IDX_DIM), const),
        ],
        out_specs=[o[1] for o in outs],
        out_shape=[o[0] for o in outs],
        scratch_shapes=[pltpu.VMEM((tm, d), bf)],
        compiler_params=pltpu.CompilerParams(
            dimension_semantics=("parallel", "parallel"),
            vmem_limit_bytes=VMEM_LIMIT_BYTES),
        name="proj",
    )(h, g, w, kvg, kg, kb)


def _sortable_to_f32(u):
    k = u ^ jnp.int32(-2 ** 31)
    bits = k ^ ((k >> 31) & jnp.int32(0x7FFFFFFF))
    return lax.bitcast_convert_type(bits, jnp.float32)


def _dsa_kernel(qa_ref, qi_ref, wi_ref, kk_ref, ckv_ref, ckvt_ref, wuk_ref, wuv_ref, bias_ref,
                o_ref, score_ref, acc_ref, m_ref, l_ref, qlat_ref, *chain_refs,
                k_sel, tk, idx_bits):
    n_chains = H_A // ATTN_HEADS_PER_CHAIN
    chain_cols = ATTN_HEADS_PER_CHAIN * BLOCK
    x_refs, p_refs, alpha_refs = (chain_refs[k * n_chains:(k + 1) * n_chains] for k in range(3))
    i = pl.program_id(1)
    blocks_per_tile = tk // BLOCK
    n_tiles = i // blocks_per_tile + 1
    f32 = jnp.float32
    bf16 = jnp.bfloat16

    lane = lax.broadcasted_iota(jnp.int32, (BLOCK, LANES), 1)
    parts = []
    for h in range(IDX_HEADS):
        pair = qi_ref[0, :, (h // 2) * LANES:(h // 2 + 1) * LANES].astype(f32)
        keep = (lane < IDX_DIM) if h % 2 == 0 else (lane >= IDX_DIM)
        parts.append(jnp.where(keep, pair, 0.0).astype(bf16))
    qm = jnp.concatenate(parts, axis=0)
    wit = wi_ref[0].T

    t_pos = i * BLOCK + lax.broadcasted_iota(jnp.int32, (tk, LANES), 1)
    row = lax.broadcasted_iota(jnp.int32, (tk, LANES), 0)

    def score_body(kt, carry):
        base = pl.multiple_of(kt * tk, tk)
        keys = kk_ref[0, pl.ds(base, tk), :]
        sc = None
        for g in range(0, IDX_HEADS, SCORE_HEADS_PER_DOT):
            d = lax.dot_general(keys, qm[g * BLOCK:(g + SCORE_HEADS_PER_DOT) * BLOCK],
                                (((1,), (1,)), ((), ())), preferred_element_type=f32)
            for a in range(SCORE_HEADS_PER_DOT):
                term = jnp.maximum(d[:, a * BLOCK:(a + 1) * BLOCK], 0.0) * wit[g + a:g + a + 1, :]
                sc = term if sc is None else sc + term
        score_ref[pl.ds(base, tk), :] = jnp.where(base + row <= t_pos, sc, -jnp.inf)
        return carry

    lax.fori_loop(0, n_tiles, score_body, 0)

    acc_rows = COUNT_ACCUMULATORS * SUBLANES

    def count(pred):
        def body(c, acc):
            base = pl.multiple_of(c * tk, tk)
            hit = jnp.where(pred(score_ref[pl.ds(base, tk), :], base), 1, 0)
            return acc + hit.reshape(tk // acc_rows, acc_rows, LANES).sum(axis=0)
        acc = lax.fori_loop(0, n_tiles, body, jnp.zeros((acc_rows, LANES), jnp.int32))
        return jnp.sum(acc, axis=0, keepdims=True)

    def bit_body(it, u):
        cand = u | lax.shift_left(jnp.int32(1), 31 - it)
        thr_c = _sortable_to_f32(cand)
        cnt = count(lambda blk, base: blk >= thr_c)
        return jnp.where(cnt >= k_sel, cand, u)

    u = lax.fori_loop(0, 32, bit_body, jnp.zeros((1, LANES), jnp.int32))
    thr = _sortable_to_f32(u)
    thr = jnp.where(thr >= F32_LOWEST, thr, F32_LOWEST)

    cnt_ge = count(lambda blk, base: blk >= thr)
    has_tie = jnp.max(jnp.where(cnt_ge > k_sel, 1.0, 0.0)) > 0.5

    @pl.when(has_tie)
    def _():
        need = k_sel - count(lambda blk, base: blk > thr)

        def idx_body(it, p):
            cand = p | lax.shift_left(jnp.int32(1), idx_bits - 1 - it)
            cnt = count(lambda blk, base: (blk == thr) & (base + row < cand))
            return jnp.where(cnt < need, cand, p)

        last = lax.fori_loop(0, idx_bits, idx_body, jnp.zeros((1, LANES), jnp.int32))

        def drop_body(c, carry):
            base = pl.multiple_of(c * tk, tk)
            blk = score_ref[pl.ds(base, tk), :]
            score_ref[pl.ds(base, tk), :] = jnp.where(
                (blk == thr) & (base + row > last), -jnp.inf, blk)
            return carry

        lax.fori_loop(0, n_tiles, drop_body, 0)

    def mask_body(c, carry):
        rows = pl.ds(pl.multiple_of(c * tk, tk), tk)
        score_ref[rows, :] = jnp.where(score_ref[rows, :] >= thr, 0.0, NEG)
        return carry

    lax.fori_loop(0, n_tiles, mask_body, 0)

    scale = HD_A ** -0.5 * LOG2E
    for h in range(H_A):
        hs = slice(h * HD_A, (h + 1) * HD_A)
        qlt = lax.dot_general(wuk_ref[:, hs], qa_ref[0, :, hs], (((1,), (1,)), ((), ())),
                              preferred_element_type=f32)
        g, a = divmod(h, ATTN_HEADS_PER_CHAIN)
        qlat_ref[g, :, a * BLOCK:(a + 1) * BLOCK] = (qlt * scale).astype(bf16)

    def tile_mask(rows):
        return jnp.concatenate([score_ref[rows, :]] * ATTN_HEADS_PER_CHAIN, axis=1)

    near_blk = jnp.maximum(i - 1, 0)
    near_base = pl.multiple_of(near_blk * BLOCK, BLOCK)
    bias_off = pl.multiple_of(jnp.where(i > 0, 0, BLOCK), BLOCK)
    near_rows = pl.ds(near_base, 2 * BLOCK)
    near_c = ckv_ref[0, near_rows, :]
    near_ct = jnp.concatenate([ckvt_ref[0, near_blk], ckvt_ref[0, near_blk + 1]], axis=1)
    near_mask = tile_mask(near_rows)
    for g in range(n_chains):
        x = (jnp.dot(near_c, qlat_ref[g], preferred_element_type=f32)
             + bias_ref[pl.ds(bias_off, 2 * BLOCK), g * chain_cols:(g + 1) * chain_cols] + near_mask)
        m_new = jnp.max(x, axis=0, keepdims=True)
        p = jnp.exp2(x - m_new)
        m_ref[g] = m_new
        l_ref[g] = jnp.sum(p, axis=0, keepdims=True)
        acc_ref[g] = jnp.dot(near_ct, p.astype(bf16), preferred_element_type=f32)
    score_ref[near_rows, :] = jnp.full((2 * BLOCK, LANES), NEG, f32)

    n_far = (i - 1 + blocks_per_tile - 1) // blocks_per_tile

    def logits_stage(kt, g):
        kt = jnp.minimum(kt, n_far - 1)
        rows = pl.ds(pl.multiple_of(kt * tk, tk), tk)
        x_refs[g][...] = jnp.dot(ckv_ref[0, rows, :], qlat_ref[g],
                                 preferred_element_type=f32) + tile_mask(rows)

    def softmax_stage(g):
        x = x_refs[g][...]
        m_old = m_ref[g]
        m_new = jnp.maximum(m_old, jnp.max(x, axis=0, keepdims=True))
        p = jnp.exp2(x - m_new)
        alpha = jnp.exp2(m_old - m_new)
        l_ref[g] = alpha * l_ref[g] + jnp.sum(p, axis=0, keepdims=True)
        m_ref[g] = m_new
        alpha_refs[g][...] = alpha
        p_refs[g][...] = p.astype(bf16)

    def values_stage(kt, g):
        kt = jnp.maximum(kt, 0)
        c_t = jnp.concatenate(
            [ckvt_ref[0, kt * blocks_per_tile + j] for j in range(blocks_per_tile)], axis=1)
        acc_ref[g] = alpha_refs[g][...] * acc_ref[g] + jnp.dot(
            c_t, p_refs[g][...], preferred_element_type=f32)

    @pl.when(n_far > 0)
    def _():
        p_refs[n_chains - 1][...] = jnp.zeros((tk, chain_cols), bf16)
        alpha_refs[n_chains - 1][...] = jnp.ones((1, chain_cols), f32)
        for g in range(PIPELINE_LOOKAHEAD):
            logits_stage(0, g)

        def tile_body(kt, carry):
            for g in range(n_chains):
                ahead, behind = g + PIPELINE_LOOKAHEAD, g - 1
                logits_stage(kt + ahead // n_chains, ahead % n_chains)
                softmax_stage(g)
                values_stage(kt + behind // n_chains, behind % n_chains)
            return carry

        lax.fori_loop(0, n_far, tile_body, 0)
        values_stage(n_far - 1, n_chains - 1)

    for h in range(H_A):
        g, a = divmod(h, ATTN_HEADS_PER_CHAIN)
        hs = slice(a * BLOCK, (a + 1) * BLOCK)
        o_lat = (acc_ref[g, :, hs] / l_ref[g, :, hs]).T.astype(bf16)
        o_ref[0, :, h * HD_A:(h + 1) * HD_A] = jnp.dot(
            o_lat, wuv_ref[:, h * HD_A:(h + 1) * HD_A],
            preferred_element_type=f32).astype(bf16)


def _dsa(qa, qi, wi, kk, ckv, ckvt, wuk, wuv, bias, tk):
    b, s, _ = qa.shape
    nb = s // BLOCK
    assert nb >= 2
    k_sel = min(TOPK_MAX, s // 4)
    idx_bits = max(1, int(math.ceil(math.log2(s))))
    cols = H_A * BLOCK
    chain_cols = ATTN_HEADS_PER_CHAIN * BLOCK
    n_chains = H_A // ATTN_HEADS_PER_CHAIN
    blk = lambda width: pl.BlockSpec((1, BLOCK, width), lambda bi, i: (bi, i, 0))
    full = lambda width: pl.BlockSpec((1, s, width), lambda bi, i: (bi, 0, 0))
    const = lambda bi, i: (0, 0)
    return pl.pallas_call(
        functools.partial(_dsa_kernel, k_sel=k_sel, tk=tk, idx_bits=idx_bits),
        grid=(b, nb),
        in_specs=[
            blk(W_QA), blk(W_QI), blk(LANES),
            full(2 * IDX_DIM), full(KV_RANK),
            pl.BlockSpec((1, nb, KV_RANK, BLOCK), lambda bi, i: (bi, 0, 0, 0)),
            pl.BlockSpec((KV_RANK, W_QA), const),
            pl.BlockSpec((KV_RANK, W_QA), const),
            pl.BlockSpec(bias.shape, const),
        ],
        out_specs=blk(W_QA),
        out_shape=jax.ShapeDtypeStruct((b, s, W_QA), jnp.bfloat16),
        scratch_shapes=[
            pltpu.VMEM((s, LANES), jnp.float32),
            pltpu.VMEM((n_chains, KV_RANK, chain_cols), jnp.float32),
            pltpu.VMEM((n_chains, 1, chain_cols), jnp.float32),
            pltpu.VMEM((n_chains, 1, chain_cols), jnp.float32),
            pltpu.VMEM((n_chains, KV_RANK, chain_cols), jnp.bfloat16),
        ]
          + [pltpu.VMEM((tk, chain_cols), jnp.float32)] * n_chains
          + [pltpu.VMEM((tk, chain_cols), jnp.bfloat16)] * n_chains
          + [pltpu.VMEM((1, chain_cols), jnp.float32)] * n_chains,
        compiler_params=pltpu.CompilerParams(
            dimension_semantics=("parallel", "arbitrary"),
            vmem_limit_bytes=VMEM_LIMIT_BYTES),
        name="dsa",
    )(qa, qi, wi, kk, ckv, ckvt, wuk, wuv, bias)


def _swa_kernel(sink_ref, q_ref, kp_ref, kc_ref, vp_ref, vc_ref, bias_ref, o_ref):
    i = pl.program_id(1)
    f32 = jnp.float32
    bf16 = jnp.bfloat16
    lane = lax.broadcasted_iota(jnp.int32, (BLOCK, LANES), 1)
    col = lax.broadcasted_iota(jnp.int32, (BLOCK, 2 * BLOCK), 1)
    prev_ok = (col >= BLOCK) | (i > 0)
    k_all = jnp.concatenate([kp_ref[0], kc_ref[0]], axis=0)
    v_all = jnp.concatenate([vp_ref[0], vc_ref[0]], axis=0)
    for pair in range(H_B // 2):
        kvh = (2 * pair) // GQA_GROUP
        kd = k_all[:, kvh * LANES:(kvh + 1) * LANES]
        vd = v_all[:, kvh * LANES:(kvh + 1) * LANES]
        q2 = q_ref[0, :, pair * LANES:(pair + 1) * LANES].astype(f32)
        halves = []
        for sub in range(2):
            h = 2 * pair + sub
            keep = (lane < HD_B) if sub == 0 else (lane >= HD_B)
            qm = jnp.where(keep, q2, 0.0).astype(bf16)
            lg = lax.dot_general(qm, kd, (((1,), (1,)), ((), ())),
                                 preferred_element_type=f32) + bias_ref[h]
            lg = jnp.where(prev_ok, lg, NEG)
            sink = sink_ref[h]
            m = jnp.maximum(jnp.max(lg, axis=-1, keepdims=True), sink)
            p = jnp.exp(lg - m)
            denom = jnp.sum(p, axis=-1, keepdims=True) + jnp.exp(sink - m)
            halves.append(jnp.dot(p.astype(bf16), vd, preferred_element_type=f32) / denom)
        o_ref[0, :, pair * LANES:(pair + 1) * LANES] = jnp.where(
            lane < HD_B, halves[0], halves[1]).astype(bf16)


def _swa(sinks, qb, kb, vb, bias):
    b, s, _ = qb.shape
    nb = s // BLOCK
    cur = lambda width: pl.BlockSpec((1, BLOCK, width), lambda bi, i: (bi, i, 0))
    prev = lambda width: pl.BlockSpec((1, BLOCK, width), lambda bi, i: (bi, jnp.maximum(i - 1, 0), 0))
    return pl.pallas_call(
        _swa_kernel,
        grid=(b, nb),
        in_specs=[
            pl.BlockSpec(memory_space=pltpu.SMEM),
            cur(W_QB), prev(2 * W_KVB), cur(2 * W_KVB), prev(2 * W_KVB), cur(2 * W_KVB),
            pl.BlockSpec(bias.shape, lambda bi, i: (0, 0, 0)),
        ],
        out_specs=cur(W_QB),
        out_shape=jax.ShapeDtypeStruct((b, s, W_QB), jnp.bfloat16),
        compiler_params=pltpu.CompilerParams(
            dimension_semantics=("parallel", "parallel"),
            vmem_limit_bytes=VMEM_LIMIT_BYTES),
        name="swa",
    )(sinks, qb, kb, kb, vb, vb, bias)


def _out_kernel(h_ref, oa_ref, ob_ref, wa_ref, wb_ref, o_ref):
    o_ref[...] = (h_ref[...]
                  + jnp.dot(oa_ref[...], wa_ref[...], preferred_element_type=jnp.float32)
                  + jnp.dot(ob_ref[...], wb_ref[...], preferred_element_type=jnp.float32))


def _out_proj(h, oa, ob, w):
    n, d = h.shape
    tm = min(OUT_TOKEN_TILE, n)
    wa_rows, wb_rows = oa.shape[1], ob.shape[1]
    assert wa_rows == wb_rows and w.shape[0] == wa_rows + wb_rows
    return pl.pallas_call(
        _out_kernel,
        grid=(n // tm,),
        in_specs=[
            pl.BlockSpec((tm, d), lambda i: (i, 0)),
            pl.BlockSpec((tm, wa_rows), lambda i: (i, 0)),
            pl.BlockSpec((tm, wb_rows), lambda i: (i, 0)),
            pl.BlockSpec((wa_rows, d), lambda i: (0, 0), pipeline_mode=pl.Buffered(1)),
            pl.BlockSpec((wb_rows, d), lambda i: (1, 0), pipeline_mode=pl.Buffered(1)),
        ],
        out_specs=pl.BlockSpec((tm, d), lambda i: (i, 0)),
        out_shape=jax.ShapeDtypeStruct((n, d), jnp.float32),
        compiler_params=pltpu.CompilerParams(
            dimension_semantics=("parallel",),
            vmem_limit_bytes=VMEM_LIMIT_BYTES),
        name="out_proj",
    )(h, oa, ob, w, w)


def _t5_bucket_np(dist):
    n = np.maximum(dist, 0)
    max_exact = N_BUCKETS // 2
    nf = np.maximum(n, 1).astype(np.float32)
    large = max_exact + (np.log(nf / np.float32(max_exact)) / np.float32(math.log(MAX_DISTANCE / max_exact))
                         * np.float32(N_BUCKETS - max_exact)).astype(np.int32)
    large = np.minimum(large, N_BUCKETS - 1)
    return np.where(n < max_exact, n, large)


def _band_distance():
    q = np.arange(BLOCK)[:, None]
    kj = np.arange(2 * BLOCK)[None, :]
    return BLOCK + q - kj


def _band_bias_kernel(table_ref, bucket_ref, o_ref):
    h = pl.program_id(0)
    bucket = bucket_ref[...]
    out = jnp.zeros(bucket.shape, jnp.float32)
    for b in range(N_BUCKETS):
        out = jnp.where(bucket == b, table_ref[b, h], out)
    o_ref[0] = out


def _band_bias(rel_bias):
    n_heads = rel_bias.shape[1]
    bucket = jnp.asarray(_t5_bucket_np(_band_distance()).astype(np.int32))
    return pl.pallas_call(
        _band_bias_kernel,
        grid=(n_heads,),
        in_specs=[
            pl.BlockSpec(memory_space=pltpu.SMEM),
            pl.BlockSpec(bucket.shape, lambda h: (0, 0)),
        ],
        out_specs=pl.BlockSpec((1,) + bucket.shape, lambda h: (h, 0, 0)),
        out_shape=jax.ShapeDtypeStruct((n_heads,) + bucket.shape, jnp.float32),
        compiler_params=pltpu.CompilerParams(dimension_semantics=("parallel",)),
        name="band_bias",
    )(rel_bias, bucket)


def _layout_w_in(w):
    splits = np.cumsum([W_QA, KV_RANK, W_QI, IDX_DIM, IDX_HEADS, W_QB, W_KVB])
    qa, ckv, qi, ki, wi, qb, kb, vb = jnp.split(w, [int(c) for c in splits], axis=-1)
    d = w.shape[0]
    dup = lambda a: jnp.concatenate(
        [a[:, kv * HD_B:(kv + 1) * HD_B] for kv in range(KVH_B) for _ in range(2)], axis=-1)
    wi_pad = jnp.concatenate([wi, jnp.zeros((d, LANES - IDX_HEADS), w.dtype)], axis=-1)
    out = jnp.concatenate([qa, ckv, qi, ki, ki, wi_pad, qb, dup(kb), dup(vb)], axis=-1)
    assert out.shape[1] == D_IN_PAD
    return out.astype(jnp.bfloat16)


def _tile_major(w):
    d, f = w.shape
    tf = FFN_HIDDEN_TILE
    return jnp.transpose(w.astype(jnp.bfloat16).reshape(d, f // tf, tf), (1, 0, 2))


def kernel(x, rel_bias, ffn1_norm, ffn1_gate, ffn1_up, ffn1_down, mix_norm, w_in, kv_norm,
           idx_k_norm_g, idx_k_norm_b, w_uk, w_uv, sinks, w_out, ffn2_norm, ffn2_gate,
           ffn2_up, ffn2_down, final_norm):
    b, s, d = x.shape
    depth = w_in.shape[0]
    bf = jnp.bfloat16
    tk = min(DSA_KEY_TILE, s)
    assert s % BLOCK == 0 and s % tk == 0

    band = _band_bias(rel_bias)
    dist = _band_distance()
    rel = (band[:H_A] - rel_bias[N_BUCKETS - 1, :H_A][:, None, None]) * LOG2E
    rel = jnp.where(jnp.asarray(dist >= 0)[None], rel, 0.0)
    rel = jnp.transpose(rel, (2, 0, 1)).reshape(2 * BLOCK, H_A * BLOCK)
    dsa_bias = jnp.concatenate([rel, jnp.zeros((BLOCK, H_A * BLOCK), jnp.float32)], axis=0)
    swa_bias = jnp.where(jnp.asarray((dist >= 0) & (dist < WINDOW))[None], band[H_A:], NEG)

    fin = final_norm.reshape(1, d)
    dup2 = lambda a: jnp.concatenate([a, a], axis=-1).reshape(1, 2 * IDX_DIM)

    h = x.reshape(b * s, d)
    for l in range(depth):
        h = _ffn(h, ffn1_norm[l].reshape(1, d), _tile_major(ffn1_gate[l]), _tile_major(ffn1_up[l]),
                 ffn1_down[l].astype(bf), fin, False)
        qa, ckv, ckvt, qi, kk, wi, qb, kb, vb = _proj(
            h.reshape(b, s, d), mix_norm[l].reshape(1, d), _layout_w_in(w_in[l]),
            kv_norm[l].reshape(1, KV_RANK), dup2(idx_k_norm_g[l]), dup2(idx_k_norm_b[l]))
        o_a = _dsa(qa, qi, wi, kk, ckv, ckvt,
                   w_uk[l].reshape(KV_RANK, W_QA).astype(bf),
                   w_uv[l].reshape(KV_RANK, W_QA).astype(bf), dsa_bias, tk)
        o_b = _swa(sinks[l], qb, kb, vb, swa_bias)
        h = _out_proj(h, o_a.reshape(b * s, W_QA), o_b.reshape(b * s, W_QB), w_out[l].astype(bf))
        h = _ffn(h, ffn2_norm[l].reshape(1, d), _tile_major(ffn2_gate[l]), _tile_major(ffn2_up[l]),
                 ffn2_down[l].astype(bf), fin, l == depth - 1)
    return h.reshape(b, s, d)
```

```python
import functools
import math

import numpy as np
import jax
import jax.numpy as jnp
from jax import lax
from jax.experimental import pallas as pl
from jax.experimental.pallas import tpu as pltpu

H_A = 8
HD_A = 128
KV_RANK = 256
IDX_HEADS = 8
IDX_DIM = 64
TOPK_MAX = 256
H_B = 16
KVH_B = 2
GQA_GROUP = H_B // KVH_B
HD_B = 64
WINDOW = 128
BLOCK = 128
N_BUCKETS = 32
MAX_DISTANCE = 128
EPS = 1e-6

LANES = 128
SUBLANES = 8
VMEM_LIMIT_BYTES = 56 * 1024 * 1024

FFN_TOKEN_TILE = 1024
FFN_HIDDEN_TILE = 512
PROJ_TOKEN_TILE = 512
OUT_TOKEN_TILE = 512
CAST_ROW_TILE = 256
DSA_KEY_TILE = 512
COUNT_ACCUMULATORS = 4
ATTN_HEADS_PER_CHAIN = 2
SCORE_HEADS_PER_DOT = 2
SCORE_LOOKAHEAD = 2
BF16_ULP_BITS = 1 << 16
HI_DTYPE = jnp.bfloat16
PIPELINE_LOOKAHEAD = 2

NEG = -1e30
F32_LOWEST = float(np.finfo(np.float32).min)
LOG2E = math.log2(math.e)

W_QA = H_A * HD_A
W_QI = IDX_HEADS * IDX_DIM
W_QB = H_B * HD_B
W_KVB = KVH_B * HD_B
OFF_QA = 0
OFF_CKV = OFF_QA + W_QA
OFF_QI = OFF_CKV + KV_RANK
OFF_KK = OFF_QI + W_QI
OFF_WI = OFF_KK + 2 * IDX_DIM
OFF_QB = OFF_WI + LANES
OFF_KB = OFF_QB + W_QB
OFF_VB = OFF_KB + 2 * W_KVB
D_IN_PAD = OFF_VB + 2 * W_KVB


def _rms(x, g):
    return x * lax.rsqrt(jnp.mean(x * x, axis=-1, keepdims=True) + EPS) * g


def _ffn_kernel(h_ref, g_ref, wg_ref, wu_ref, wd_ref, fg_ref, o_ref, xn_ref, *, final_norm):
    j = pl.program_id(1)

    @pl.when(j == 0)
    def _():
        x = h_ref[...]
        xn_ref[...] = _rms(x, g_ref[...]).astype(jnp.bfloat16)
        o_ref[...] = x

    xn = xn_ref[...]
    gate = jnp.dot(xn, wg_ref[0], preferred_element_type=jnp.float32)
    up = jnp.dot(xn, wu_ref[0], preferred_element_type=jnp.float32)
    act = (gate * jax.nn.sigmoid(gate) * (0.5 * up)).astype(jnp.bfloat16)
    o_ref[...] += jnp.dot(act, wd_ref[...], preferred_element_type=jnp.float32)

    if final_norm:
        @pl.when(j == pl.num_programs(1) - 1)
        def _():
            o_ref[...] = _rms(o_ref[...], fg_ref[...])


def _ffn(h, g, wg, wu, wd, fg, final_norm):
    n, d = h.shape
    tf = FFN_HIDDEN_TILE
    f = wg.shape[0] * tf
    assert wg.shape == wu.shape == (f // tf, d, tf) and wd.shape == (f, d)
    tm = min(FFN_TOKEN_TILE, n)
    return pl.pallas_call(
        functools.partial(_ffn_kernel, final_norm=final_norm),
        grid=(n // tm, f // tf),
        in_specs=[
            pl.BlockSpec((tm, d), lambda i, j: (i, 0)),
            pl.BlockSpec((1, d), lambda i, j: (0, 0)),
            pl.BlockSpec((1, d, tf), lambda i, j: (j, 0, 0)),
            pl.BlockSpec((1, d, tf), lambda i, j: (j, 0, 0)),
            pl.BlockSpec((tf, d), lambda i, j: (j, 0)),
            pl.BlockSpec((1, d), lambda i, j: (0, 0)),
        ],
        out_specs=pl.BlockSpec((tm, d), lambda i, j: (i, 0)),
        out_shape=jax.ShapeDtypeStruct((n, d), jnp.float32),
        scratch_shapes=[pltpu.VMEM((tm, d), jnp.bfloat16)],
        compiler_params=pltpu.CompilerParams(
            dimension_semantics=("parallel", "arbitrary"),
            vmem_limit_bytes=VMEM_LIMIT_BYTES),
        name="ffn",
    )(h, g, wg, wu, wd, fg)


def _proj_kernel(h_ref, g_ref, w_ref, kvg_ref, kg_ref, kb_ref,
                 qa_o, ckv_o, ckvt_o, qi_o, kk_o, wi_o, qb_o, kb_o, vb_o, xn_ref):
    xn_ref[...] = _rms(h_ref[0], g_ref[...]).astype(jnp.bfloat16)

    def seg(lo, width):
        return jnp.dot(xn_ref[...], w_ref[:, lo:lo + width], preferred_element_type=jnp.float32)

    qa_o[0] = seg(OFF_QA, W_QA).astype(jnp.bfloat16)

    c = _rms(seg(OFF_CKV, KV_RANK), kvg_ref[...])
    ckv_o[0] = c.astype(jnp.bfloat16)
    ct = c.T
    for t in range(ckvt_o.shape[1]):
        ckvt_o[0, t] = ct[:, t * BLOCK:(t + 1) * BLOCK].astype(jnp.bfloat16)

    qi_o[0] = seg(OFF_QI, W_QI).astype(jnp.bfloat16)

    k = seg(OFF_KK, 2 * IDX_DIM)
    mu = jnp.mean(k, axis=-1, keepdims=True)
    kc = k - mu
    var = jnp.mean(kc * kc, axis=-1, keepdims=True)
    kk_o[0] = (kc * lax.rsqrt(var + EPS) * kg_ref[...] + kb_ref[...]).astype(jnp.bfloat16)

    wi_o[0] = seg(OFF_WI, LANES) * (IDX_HEADS ** -0.5 * IDX_DIM ** -0.5)
    qb_o[0] = (seg(OFF_QB, W_QB) * (HD_B ** -0.5)).astype(jnp.bfloat16)
    kb_o[0] = seg(OFF_KB, 2 * W_KVB).astype(jnp.bfloat16)
    vb_o[0] = seg(OFF_VB, 2 * W_KVB).astype(jnp.bfloat16)


def _proj(h, g, w, kvg, kg, kb):
    b, s, d = h.shape
    tm = min(PROJ_TOKEN_TILE, s)
    bf = jnp.bfloat16

    def tok(width, dtype):
        return (jax.ShapeDtypeStruct((b, s, width), dtype),
                pl.BlockSpec((1, tm, width), lambda bi, ti: (bi, ti, 0)))

    outs = [
        tok(W_QA, bf),
        tok(KV_RANK, bf),
        (jax.ShapeDtypeStruct((b, s // BLOCK, KV_RANK, BLOCK), bf),
         pl.BlockSpec((1, tm // BLOCK, KV_RANK, BLOCK), lambda bi, ti: (bi, ti, 0, 0))),
        tok(W_QI, bf),
        tok(2 * IDX_DIM, bf),
        tok(LANES, jnp.float32),
        tok(W_QB, bf),
        tok(2 * W_KVB, bf),
        tok(2 * W_KVB, bf),
    ]
    const = lambda bi, ti: (0, 0)
    return pl.pallas_call(
        _proj_kernel,
        grid=(b, s // tm),
        in_specs=[
            pl.BlockSpec((1, tm, d), lambda bi, ti: (bi, ti, 0)),
            pl.BlockSpec((1, d), const),
            pl.BlockSpec((d, D_IN_PAD), const, pipeline_mode=pl.Buffered(1)),
            pl.BlockSpec((1, KV_RANK), const),
            pl.BlockSpec((1, 2 * IDX_DIM), const),
            pl.BlockSpec((1, 2 * IDX_DIM), const),
        ],
        out_specs=[o[1] for o in outs],
        out_shape=[o[0] for o in outs],
        scratch_shapes=[pltpu.VMEM((tm, d), bf)],
        compiler_params=pltpu.CompilerParams(
            dimension_semantics=("parallel", "parallel"),
            vmem_limit_bytes=VMEM_LIMIT_BYTES),
        name="proj",
    )(h, g, w, kvg, kg, kb)


def _sortable_to_f32(u):
    k = u ^ jnp.int32(-2 ** 31)
    bits = k ^ ((k >> 31) & jnp.int32(0x7FFFFFFF))
    return lax.bitcast_convert_type(bits, jnp.float32)


def _dsa_kernel(qa_ref, qi_ref, wi_ref, kk_ref, ckv_ref, ckvt_ref, wuk_ref, wuv_ref, bias_ref,
                o_ref, score_ref, hi_ref, acc_ref, m_ref, l_ref, qlat_ref, *pipeline_refs,
                k_sel, tk, idx_bits, n_tiles_max):
    n_chains = H_A // ATTN_HEADS_PER_CHAIN
    chain_cols = ATTN_HEADS_PER_CHAIN * BLOCK
    d_refs = pipeline_refs[:IDX_HEADS // SCORE_HEADS_PER_DOT]
    chain_refs = pipeline_refs[len(d_refs):]
    x_refs, p_refs, alpha_refs = (chain_refs[k * n_chains:(k + 1) * n_chains] for k in range(3))
    i = pl.program_id(1)
    blocks_per_tile = tk // BLOCK
    n_tiles = i // blocks_per_tile + 1
    f32 = jnp.float32
    bf16 = jnp.bfloat16

    lane = lax.broadcasted_iota(jnp.int32, (BLOCK, LANES), 1)
    parts = []
    for h in range(IDX_HEADS):
        pair = qi_ref[0, :, (h // 2) * LANES:(h // 2 + 1) * LANES].astype(f32)
        keep = (lane < IDX_DIM) if h % 2 == 0 else (lane >= IDX_DIM)
        parts.append(jnp.where(keep, pair, 0.0).astype(bf16))
    qm = jnp.concatenate(parts, axis=0)
    wit = wi_ref[0].T

    t_pos = i * BLOCK + lax.broadcasted_iota(jnp.int32, (tk, LANES), 1)
    row = lax.broadcasted_iota(jnp.int32, (tk, LANES), 0)

    n_groups = IDX_HEADS // SCORE_HEADS_PER_DOT

    def score_dot(kt, g):
        kt = jnp.minimum(kt, n_tiles - 1)
        rows = pl.ds(pl.multiple_of(kt * tk, tk), tk)
        q_rows = slice(g * SCORE_HEADS_PER_DOT * BLOCK, (g + 1) * SCORE_HEADS_PER_DOT * BLOCK)
        d_refs[g][...] = lax.dot_general(kk_ref[0, rows, :], qm[q_rows], (((1,), (1,)), ((), ())),
                                         preferred_element_type=f32)

    def score_terms(g):
        d = d_refs[g][...]
        out = None
        for a in range(SCORE_HEADS_PER_DOT):
            h = g * SCORE_HEADS_PER_DOT + a
            term = jnp.maximum(d[:, a * BLOCK:(a + 1) * BLOCK], 0.0) * wit[h:h + 1, :]
            out = term if out is None else out + term
        return out

    def score_finish(kt, sc):
        base = pl.multiple_of(kt * tk, tk)
        sc = jnp.where(base + row <= t_pos, sc, -jnp.inf)
        score_ref[pl.ds(base, tk), :] = sc
        near = sc.astype(HI_DTYPE).astype(f32)
        inward = lax.bitcast_convert_type(
            lax.bitcast_convert_type(near, jnp.int32) - BF16_ULP_BITS, f32)
        hi_ref[pl.ds(base, tk), :] = jnp.where(
            jnp.abs(near) > jnp.abs(sc), inward, near).astype(HI_DTYPE)

    for g in range(SCORE_LOOKAHEAD):
        score_dot(0, g)

    def score_body(kt, carry):
        sc = None
        for g in range(n_groups):
            ahead = g + SCORE_LOOKAHEAD
            score_dot(kt + ahead // n_groups, ahead % n_groups)
            term = score_terms(g)
            sc = term if sc is None else sc + term
        score_finish(kt, sc)
        return carry

    lax.fori_loop(0, n_tiles, score_body, 0)

    acc_rows = COUNT_ACCUMULATORS * SUBLANES
    packed = 2 * SUBLANES

    def count(pred):
        def body(c, acc):
            base = pl.multiple_of(c * tk, tk)
            hit = jnp.where(pred(score_ref[pl.ds(base, tk), :], base), 1, 0)
            return acc + hit.reshape(tk // acc_rows, acc_rows, LANES).sum(axis=0)
        acc = lax.fori_loop(0, n_tiles, body, jnp.zeros((acc_rows, LANES), jnp.int32))
        return jnp.sum(acc, axis=0, keepdims=True)

    def count_hi(thr_hi):
        one = jnp.ones((tk // packed, packed, LANES), HI_DTYPE)
        zero = jnp.zeros((tk // packed, packed, LANES), HI_DTYPE)

        def body(c, acc):
            base = pl.multiple_of(c * tk, tk)
            blk = hi_ref[pl.ds(base, tk), :].reshape(tk // packed, packed, LANES)
            hit = jnp.where(blk >= thr_hi[None], one, zero).reshape(
                tk // (packed * COUNT_ACCUMULATORS), COUNT_ACCUMULATORS, packed, LANES)
            for r in range(hit.shape[0]):
                acc = acc + hit[r]
            return acc

        acc = lax.fori_loop(0, n_tiles, body,
                            jnp.zeros((COUNT_ACCUMULATORS, packed, LANES), HI_DTYPE))
        return jnp.sum(acc.astype(f32).reshape(COUNT_ACCUMULATORS * packed, LANES),
                       axis=0, keepdims=True)

    def hi_bit_body(it, u):
        cand = u | lax.shift_left(jnp.int32(1), 31 - it)
        bits = lax.bitcast_convert_type(_sortable_to_f32(cand), jnp.int32) & jnp.int32(-BF16_ULP_BITS)
        thr_hi = lax.bitcast_convert_type(jnp.broadcast_to(bits, (packed, LANES)), f32).astype(HI_DTYPE)
        return jnp.where(count_hi(thr_hi) >= k_sel, cand, u)

    def bit_body(it, u):
        cand = u | lax.shift_left(jnp.int32(1), 31 - it)
        thr_c = _sortable_to_f32(cand)
        cnt = count(lambda blk, base: blk >= thr_c)
        return jnp.where(cnt >= k_sel, cand, u)

    assert n_tiles_max * tk // (packed * COUNT_ACCUMULATORS) <= 256
    u = lax.fori_loop(0, 16, hi_bit_body, jnp.zeros((1, LANES), jnp.int32))
    u = lax.fori_loop(16, 32, bit_body, u)
    thr = _sortable_to_f32(u)
    thr = jnp.where(thr >= F32_LOWEST, thr, F32_LOWEST)

    cnt_ge = count(lambda blk, base: blk >= thr)
    has_tie = jnp.max(jnp.where(cnt_ge > k_sel, 1.0, 0.0)) > 0.5

    @pl.when(has_tie)
    def _():
        need = k_sel - count(lambda blk, base: blk > thr)

        def idx_body(it, p):
            cand = p | lax.shift_left(jnp.int32(1), idx_bits - 1 - it)
            cnt = count(lambda blk, base: (blk == thr) & (base + row < cand))
            return jnp.where(cnt < need, cand, p)

        last = lax.fori_loop(0, idx_bits, idx_body, jnp.zeros((1, LANES), jnp.int32))

        def drop_body(c, carry):
            base = pl.multiple_of(c * tk, tk)
            blk = score_ref[pl.ds(base, tk), :]
            score_ref[pl.ds(base, tk), :] = jnp.where(
                (blk == thr) & (base + row > last), -jnp.inf, blk)
            return carry

        lax.fori_loop(0, n_tiles, drop_body, 0)

    def mask_body(c, carry):
        rows = pl.ds(pl.multiple_of(c * tk, tk), tk)
        score_ref[rows, :] = jnp.where(score_ref[rows, :] >= thr, 0.0, NEG)
        return carry

    lax.fori_loop(0, n_tiles, mask_body, 0)

    scale = HD_A ** -0.5 * LOG2E
    for h in range(H_A):
        hs = slice(h * HD_A, (h + 1) * HD_A)
        qlt = lax.dot_general(wuk_ref[:, hs], qa_ref[0, :, hs], (((1,), (1,)), ((), ())),
                              preferred_element_type=f32)
        g, a = divmod(h, ATTN_HEADS_PER_CHAIN)
        qlat_ref[g, :, a * BLOCK:(a + 1) * BLOCK] = (qlt * scale).astype(bf16)

    def tile_mask(rows):
        return jnp.concatenate([score_ref[rows, :]] * ATTN_HEADS_PER_CHAIN, axis=1)

    near_blk = jnp.maximum(i - 1, 0)
    near_base = pl.multiple_of(near_blk * BLOCK, BLOCK)
    bias_off = pl.multiple_of(jnp.where(i > 0, 0, BLOCK), BLOCK)
    near_rows = pl.ds(near_base, 2 * BLOCK)
    near_c = ckv_ref[0, near_rows, :]
    near_ct = jnp.concatenate([ckvt_ref[0, near_blk], ckvt_ref[0, near_blk + 1]], axis=1)
    near_mask = tile_mask(near_rows)
    for g in range(n_chains):
        x = (jnp.dot(near_c, qlat_ref[g], preferred_element_type=f32)
             + bias_ref[pl.ds(bias_off, 2 * BLOCK), g * chain_cols:(g + 1) * chain_cols] + near_mask)
        m_new = jnp.max(x, axis=0, keepdims=True)
        p = jnp.exp2(x - m_new)
        m_ref[g] = m_new
        l_ref[g] = jnp.sum(p, axis=0, keepdims=True)
        acc_ref[g] = jnp.dot(near_ct, p.astype(bf16), preferred_element_type=f32)
    score_ref[near_rows, :] = jnp.full((2 * BLOCK, LANES), NEG, f32)

    n_far = (i - 1 + blocks_per_tile - 1) // blocks_per_tile

    def logits_stage(kt, g):
        kt = jnp.minimum(kt, n_far - 1)
        rows = pl.ds(pl.multiple_of(kt * tk, tk), tk)
        x_refs[g][...] = jnp.dot(ckv_ref[0, rows, :], qlat_ref[g],
                                 preferred_element_type=f32) + tile_mask(rows)

    def softmax_stage(g):
        x = x_refs[g][...]
        m_old = m_ref[g]
        m_new = jnp.maximum(m_old, jnp.max(x, axis=0, keepdims=True))
        p = jnp.exp2(x - m_new)
        alpha = jnp.exp2(m_old - m_new)
        l_ref[g] = alpha * l_ref[g] + jnp.sum(p, axis=0, keepdims=True)
        m_ref[g] = m_new
        alpha_refs[g][...] = alpha
        p_refs[g][...] = p.astype(bf16)

    def values_stage(kt, g):
        kt = jnp.maximum(kt, 0)
        c_t = jnp.concatenate(
            [ckvt_ref[0, kt * blocks_per_tile + j] for j in range(blocks_per_tile)], axis=1)
        acc_ref[g] = alpha_refs[g][...] * acc_ref[g] + jnp.dot(
            c_t, p_refs[g][...], preferred_element_type=f32)

    @pl.when(n_far > 0)
    def _():
        p_refs[n_chains - 1][...] = jnp.zeros((tk, chain_cols), bf16)
        alpha_refs[n_chains - 1][...] = jnp.ones((1, chain_cols), f32)
        for g in range(PIPELINE_LOOKAHEAD):
            logits_stage(0, g)

        def tile_body(kt, carry):
            for g in range(n_chains):
                ahead, behind = g + PIPELINE_LOOKAHEAD, g - 1
                logits_stage(kt + ahead // n_chains, ahead % n_chains)
                softmax_stage(g)
                values_stage(kt + behind // n_chains, behind % n_chains)
            return carry

        lax.fori_loop(0, n_far, tile_body, 0)
        values_stage(n_far - 1, n_chains - 1)

    for h in range(H_A):
        g, a = divmod(h, ATTN_HEADS_PER_CHAIN)
        hs = slice(a * BLOCK, (a + 1) * BLOCK)
        o_lat = (acc_ref[g, :, hs] / l_ref[g, :, hs]).T.astype(bf16)
        o_ref[0, :, h * HD_A:(h + 1) * HD_A] = jnp.dot(
            o_lat, wuv_ref[:, h * HD_A:(h + 1) * HD_A],
            preferred_element_type=f32).astype(bf16)


def _dsa(qa, qi, wi, kk, ckv, ckvt, wuk, wuv, bias, tk):
    b, s, _ = qa.shape
    nb = s // BLOCK
    assert nb >= 2
    k_sel = min(TOPK_MAX, s // 4)
    idx_bits = max(1, int(math.ceil(math.log2(s))))
    cols = H_A * BLOCK
    chain_cols = ATTN_HEADS_PER_CHAIN * BLOCK
    n_chains = H_A // ATTN_HEADS_PER_CHAIN
    blk = lambda width: pl.BlockSpec((1, BLOCK, width), lambda bi, i: (bi, i, 0))
    full = lambda width: pl.BlockSpec((1, s, width), lambda bi, i: (bi, 0, 0))
    const = lambda bi, i: (0, 0)
    return pl.pallas_call(
        functools.partial(_dsa_kernel, k_sel=k_sel, tk=tk, idx_bits=idx_bits, n_tiles_max=s // tk),
        grid=(b, nb),
        in_specs=[
            blk(W_QA), blk(W_QI), blk(LANES),
            full(2 * IDX_DIM), full(KV_RANK),
            pl.BlockSpec((1, nb, KV_RANK, BLOCK), lambda bi, i: (bi, 0, 0, 0)),
            pl.BlockSpec((KV_RANK, W_QA), const),
            pl.BlockSpec((KV_RANK, W_QA), const),
            pl.BlockSpec(bias.shape, const),
        ],
        out_specs=blk(W_QA),
        out_shape=jax.ShapeDtypeStruct((b, s, W_QA), jnp.bfloat16),
        scratch_shapes=[
            pltpu.VMEM((s, LANES), jnp.float32),
            pltpu.VMEM((s, LANES), HI_DTYPE),
            pltpu.VMEM((n_chains, KV_RANK, chain_cols), jnp.float32),
            pltpu.VMEM((n_chains, 1, chain_cols), jnp.float32),
            pltpu.VMEM((n_chains, 1, chain_cols), jnp.float32),
            pltpu.VMEM((n_chains, KV_RANK, chain_cols), jnp.bfloat16),
        ]
          + [pltpu.VMEM((tk, SCORE_HEADS_PER_DOT * BLOCK), jnp.float32)] * (IDX_HEADS // SCORE_HEADS_PER_DOT)
          + [pltpu.VMEM((tk, chain_cols), jnp.float32)] * n_chains
          + [pltpu.VMEM((tk, chain_cols), jnp.bfloat16)] * n_chains
          + [pltpu.VMEM((1, chain_cols), jnp.float32)] * n_chains,
        compiler_params=pltpu.CompilerParams(
            dimension_semantics=("parallel", "arbitrary"),
            vmem_limit_bytes=VMEM_LIMIT_BYTES),
        name="dsa",
    )(qa, qi, wi, kk, ckv, ckvt, wuk, wuv, bias)


def _swa_kernel(sink_ref, q_ref, kp_ref, kc_ref, vp_ref, vc_ref, bias_ref, o_ref):
    i = pl.program_id(1)
    f32 = jnp.float32
    bf16 = jnp.bfloat16
    lane = lax.broadcasted_iota(jnp.int32, (BLOCK, LANES), 1)
    col = lax.broadcasted_iota(jnp.int32, (BLOCK, 2 * BLOCK), 1)
    prev_ok = (col >= BLOCK) | (i > 0)
    k_all = jnp.concatenate([kp_ref[0], kc_ref[0]], axis=0)
    v_all = jnp.concatenate([vp_ref[0], vc_ref[0]], axis=0)
    for pair in range(H_B // 2):
        kvh = (2 * pair) // GQA_GROUP
        kd = k_all[:, kvh * LANES:(kvh + 1) * LANES]
        vd = v_all[:, kvh * LANES:(kvh + 1) * LANES]
        q2 = q_ref[0, :, pair * LANES:(pair + 1) * LANES].astype(f32)
        halves = []
        for sub in range(2):
            h = 2 * pair + sub
            keep = (lane < HD_B) if sub == 0 else (lane >= HD_B)
            qm = jnp.where(keep, q2, 0.0).astype(bf16)
            lg = lax.dot_general(qm, kd, (((1,), (1,)), ((), ())),
                                 preferred_element_type=f32) + bias_ref[h]
            lg = jnp.where(prev_ok, lg, NEG)
            sink = sink_ref[h]
            m = jnp.maximum(jnp.max(lg, axis=-1, keepdims=True), sink)
            p = jnp.exp(lg - m)
            denom = jnp.sum(p, axis=-1, keepdims=True) + jnp.exp(sink - m)
            halves.append(jnp.dot(p.astype(bf16), vd, preferred_element_type=f32) / denom)
        o_ref[0, :, pair * LANES:(pair + 1) * LANES] = jnp.where(
            lane < HD_B, halves[0], halves[1]).astype(bf16)


def _swa(sinks, qb, kb, vb, bias):
    b, s, _ = qb.shape
    nb = s // BLOCK
    cur = lambda width: pl.BlockSpec((1, BLOCK, width), lambda bi, i: (bi, i, 0))
    prev = lambda width: pl.BlockSpec((1, BLOCK, width), lambda bi, i: (bi, jnp.maximum(i - 1, 0), 0))
    return pl.pallas_call(
        _swa_kernel,
        grid=(b, nb),
        in_specs=[
            pl.BlockSpec(memory_space=pltpu.SMEM),
            cur(W_QB), prev(2 * W_KVB), cur(2 * W_KVB), prev(2 * W_KVB), cur(2 * W_KVB),
            pl.BlockSpec(bias.shape, lambda bi, i: (0, 0, 0)),
        ],
        out_specs=cur(W_QB),
        out_shape=jax.ShapeDtypeStruct((b, s, W_QB), jnp.bfloat16),
        compiler_params=pltpu.CompilerParams(
            dimension_semantics=("parallel", "parallel"),
            vmem_limit_bytes=VMEM_LIMIT_BYTES),
        name="swa",
    )(sinks, qb, kb, kb, vb, vb, bias)


def _out_kernel(h_ref, oa_ref, ob_ref, wa_ref, wb_ref, o_ref):
    o_ref[...] = (h_ref[...]
                  + jnp.dot(oa_ref[...], wa_ref[...], preferred_element_type=jnp.float32)
                  + jnp.dot(ob_ref[...], wb_ref[...], preferred_element_type=jnp.float32))


def _out_proj(h, oa, ob, w):
    n, d = h.shape
    tm = min(OUT_TOKEN_TILE, n)
    wa_rows, wb_rows = oa.shape[1], ob.shape[1]
    assert wa_rows == wb_rows and w.shape[0] == wa_rows + wb_rows
    return pl.pallas_call(
        _out_kernel,
        grid=(n // tm,),
        in_specs=[
            pl.BlockSpec((tm, d), lambda i: (i, 0)),
            pl.BlockSpec((tm, wa_rows), lambda i: (i, 0)),
            pl.BlockSpec((tm, wb_rows), lambda i: (i, 0)),
            pl.BlockSpec((wa_rows, d), lambda i: (0, 0), pipeline_mode=pl.Buffered(1)),
            pl.BlockSpec((wb_rows, d), lambda i: (1, 0), pipeline_mode=pl.Buffered(1)),
        ],
        out_specs=pl.BlockSpec((tm, d), lambda i: (i, 0)),
        out_shape=jax.ShapeDtypeStruct((n, d), jnp.float32),
        compiler_params=pltpu.CompilerParams(
            dimension_semantics=("parallel",),
            vmem_limit_bytes=VMEM_LIMIT_BYTES),
        name="out_proj",
    )(h, oa, ob, w, w)


def _t5_bucket_np(dist):
    n = np.maximum(dist, 0)
    max_exact = N_BUCKETS // 2
    nf = np.maximum(n, 1).astype(np.float32)
    large = max_exact + (np.log(nf / np.float32(max_exact)) / np.float32(math.log(MAX_DISTANCE / max_exact))
                         * np.float32(N_BUCKETS - max_exact)).astype(np.int32)
    large = np.minimum(large, N_BUCKETS - 1)
    return np.where(n < max_exact, n, large)


def _band_distance():
    q = np.arange(BLOCK)[:, None]
    kj = np.arange(2 * BLOCK)[None, :]
    return BLOCK + q - kj


def _band_bias_kernel(table_ref, bucket_ref, o_ref):
    h = pl.program_id(0)
    bucket = bucket_ref[...]
    out = jnp.zeros(bucket.shape, jnp.float32)
    for b in range(N_BUCKETS):
        out = jnp.where(bucket == b, table_ref[b, h], out)
    o_ref[0] = out


def _band_bias(rel_bias):
    n_heads = rel_bias.shape[1]
    bucket = jnp.asarray(_t5_bucket_np(_band_distance()).astype(np.int32))
    return pl.pallas_call(
        _band_bias_kernel,
        grid=(n_heads,),
        in_specs=[
            pl.BlockSpec(memory_space=pltpu.SMEM),
            pl.BlockSpec(bucket.shape, lambda h: (0, 0)),
        ],
        out_specs=pl.BlockSpec((1,) + bucket.shape, lambda h: (h, 0, 0)),
        out_shape=jax.ShapeDtypeStruct((n_heads,) + bucket.shape, jnp.float32),
        compiler_params=pltpu.CompilerParams(dimension_semantics=("parallel",)),
        name="band_bias",
    )(rel_bias, bucket)


def _layout_w_in(w):
    splits = np.cumsum([W_QA, KV_RANK, W_QI, IDX_DIM, IDX_HEADS, W_QB, W_KVB])
    qa, ckv, qi, ki, wi, qb, kb, vb = jnp.split(w, [int(c) for c in splits], axis=-1)
    d = w.shape[0]
    dup = lambda a: jnp.concatenate(
        [a[:, kv * HD_B:(kv + 1) * HD_B] for kv in range(KVH_B) for _ in range(2)], axis=-1)
    wi_pad = jnp.concatenate([wi, jnp.zeros((d, LANES - IDX_HEADS), w.dtype)], axis=-1)
    out = jnp.concatenate([qa, ckv, qi, ki, ki, wi_pad, qb, dup(kb), dup(vb)], axis=-1)
    assert out.shape[1] == D_IN_PAD
    return out.astype(jnp.bfloat16)


def _tile_major_kernel(w_ref, o_ref):
    tf = o_ref.shape[2]
    for j in range(o_ref.shape[0]):
        o_ref[j] = w_ref[:, j * tf:(j + 1) * tf].astype(jnp.bfloat16)


def _tile_major(w):
    d, f = w.shape
    tf = FFN_HIDDEN_TILE
    rows = CAST_ROW_TILE
    return pl.pallas_call(
        _tile_major_kernel,
        grid=(d // rows,),
        in_specs=[pl.BlockSpec((rows, f), lambda i: (i, 0))],
        out_specs=pl.BlockSpec((f // tf, rows, tf), lambda i: (0, i, 0)),
        out_shape=jax.ShapeDtypeStruct((f // tf, d, tf), jnp.bfloat16),
        compiler_params=pltpu.CompilerParams(
            dimension_semantics=("parallel",), vmem_limit_bytes=VMEM_LIMIT_BYTES),
        name="tile_major",
    )(w)


def kernel(x, rel_bias, ffn1_norm, ffn1_gate, ffn1_up, ffn1_down, mix_norm, w_in, kv_norm,
           idx_k_norm_g, idx_k_norm_b, w_uk, w_uv, sinks, w_out, ffn2_norm, ffn2_gate,
           ffn2_up, ffn2_down, final_norm):
    b, s, d = x.shape
    depth = w_in.shape[0]
    bf = jnp.bfloat16
    tk = min(DSA_KEY_TILE, s)
    assert s % BLOCK == 0 and s % tk == 0

    band = _band_bias(rel_bias)
    dist = _band_distance()
    rel = (band[:H_A] - rel_bias[N_BUCKETS - 1, :H_A][:, None, None]) * LOG2E
    rel = jnp.where(jnp.asarray(dist >= 0)[None], rel, 0.0)
    rel = jnp.transpose(rel, (2, 0, 1)).reshape(2 * BLOCK, H_A * BLOCK)
    dsa_bias = jnp.concatenate([rel, jnp.zeros((BLOCK, H_A * BLOCK), jnp.float32)], axis=0)
    swa_bias = jnp.where(jnp.asarray((dist >= 0) & (dist < WINDOW))[None], band[H_A:], NEG)

    fin = final_norm.reshape(1, d)
    dup2 = lambda a: jnp.concatenate([a, a], axis=-1).reshape(1, 2 * IDX_DIM)

    h = x.reshape(b * s, d)
    for l in range(depth):
        h = _ffn(h, ffn1_norm[l].reshape(1, d), _tile_major(ffn1_gate[l]), _tile_major(ffn1_up[l]),
                 ffn1_down[l].astype(bf), fin, False)
        qa, ckv, ckvt, qi, kk, wi, qb, kb, vb = _proj(
            h.reshape(b, s, d), mix_norm[l].reshape(1, d), _layout_w_in(w_in[l]),
            kv_norm[l].reshape(1, KV_RANK), dup2(idx_k_norm_g[l]), dup2(idx_k_norm_b[l]))
        o_a = _dsa(qa, qi, wi, kk, ckv, ckvt,
                   w_uk[l].reshape(KV_RANK, W_QA).astype(bf),
                   w_uv[l].reshape(KV_RANK, W_QA).astype(bf), dsa_bias, tk)
        o_b = _swa(sinks[l], qb, kb, vb, swa_bias)
        h = _out_proj(h, o_a.reshape(b * s, W_QA), o_b.reshape(b * s, W_QB), w_out[l].astype(bf))
        h = _ffn(h, ffn2_norm[l].reshape(1, d), _tile_major(ffn2_gate[l]), _tile_major(ffn2_up[l]),
                 ffn2_down[l].astype(bf), fin, l == depth - 1)
    return h.reshape(b, s, d)
```

```python
import functools
import math

import numpy as np
import jax
import jax.numpy as jnp
from jax import lax
from jax.experimental import pallas as pl
from jax.experimental.pallas import tpu as pltpu

H_A = 8
HD_A = 128
KV_RANK = 256
IDX_HEADS = 8
IDX_DIM = 64
TOPK_MAX = 256
H_B = 16
KVH_B = 2
GQA_GROUP = H_B // KVH_B
HD_B = 64
WINDOW = 128
BLOCK = 128
N_BUCKETS = 32
MAX_DISTANCE = 128
EPS = 1e-6

LANES = 128
SUBLANES = 8
VMEM_LIMIT_BYTES = 56 * 1024 * 1024

FFN_TOKEN_TILE = 1024
FFN_HIDDEN_TILE = 512
PROJ_TOKEN_TILE = 512
OUT_TOKEN_TILE = 512
CAST_ROW_TILE = 256
DSA_KEY_TILE = 512
COUNT_ACCUMULATORS = 4
ATTN_HEADS_PER_CHAIN = 2
SCORE_HEADS_PER_DOT = 2
SCAN_TILES = 2
PIPELINE_LOOKAHEAD = 2

NEG = -1e30
F32_LOWEST = float(np.finfo(np.float32).min)
LOG2E = math.log2(math.e)

W_QA = H_A * HD_A
W_QI = IDX_HEADS * IDX_DIM
W_QB = H_B * HD_B
W_KVB = KVH_B * HD_B
OFF_QA = 0
OFF_CKV = OFF_QA + W_QA
OFF_QI = OFF_CKV + KV_RANK
OFF_KK = OFF_QI + W_QI
OFF_WI = OFF_KK + 2 * IDX_DIM
OFF_QB = OFF_WI + LANES
OFF_KB = OFF_QB + W_QB
OFF_VB = OFF_KB + 2 * W_KVB
D_IN_PAD = OFF_VB + 2 * W_KVB


def _rms(x, g):
    return x * lax.rsqrt(jnp.mean(x * x, axis=-1, keepdims=True) + EPS) * g


def _ffn_kernel(h_ref, g_ref, wg_ref, wu_ref, wd_ref, fg_ref, o_ref, xn_ref, *, final_norm):
    j = pl.program_id(1)

    @pl.when(j == 0)
    def _():
        x = h_ref[...]
        xn_ref[...] = _rms(x, g_ref[...]).astype(jnp.bfloat16)
        o_ref[...] = x

    xn = xn_ref[...]
    gate = jnp.dot(xn, wg_ref[0], preferred_element_type=jnp.float32)
    up = jnp.dot(xn, wu_ref[0], preferred_element_type=jnp.float32)
    act = (gate * jax.nn.sigmoid(gate) * (0.5 * up)).astype(jnp.bfloat16)
    o_ref[...] += jnp.dot(act, wd_ref[...], preferred_element_type=jnp.float32)

    if final_norm:
        @pl.when(j == pl.num_programs(1) - 1)
        def _():
            o_ref[...] = _rms(o_ref[...], fg_ref[...])


def _ffn(h, g, wg, wu, wd, fg, final_norm):
    n, d = h.shape
    tf = FFN_HIDDEN_TILE
    f = wg.shape[0] * tf
    assert wg.shape == wu.shape == (f // tf, d, tf) and wd.shape == (f, d)
    tm = min(FFN_TOKEN_TILE, n)
    return pl.pallas_call(
        functools.partial(_ffn_kernel, final_norm=final_norm),
        grid=(n // tm, f // tf),
        in_specs=[
            pl.BlockSpec((tm, d), lambda i, j: (i, 0)),
            pl.BlockSpec((1, d), lambda i, j: (0, 0)),
            pl.BlockSpec((1, d, tf), lambda i, j: (j, 0, 0)),
            pl.BlockSpec((1, d, tf), lambda i, j: (j, 0, 0)),
            pl.BlockSpec((tf, d), lambda i, j: (j, 0)),
            pl.BlockSpec((1, d), lambda i, j: (0, 0)),
        ],
        out_specs=pl.BlockSpec((tm, d), lambda i, j: (i, 0)),
        out_shape=jax.ShapeDtypeStruct((n, d), jnp.float32),
        scratch_shapes=[pltpu.VMEM((tm, d), jnp.bfloat16)],
        compiler_params=pltpu.CompilerParams(
            dimension_semantics=("parallel", "arbitrary"),
            vmem_limit_bytes=VMEM_LIMIT_BYTES),
        name="ffn",
    )(h, g, wg, wu, wd, fg)


def _proj_kernel(h_ref, g_ref, w_ref, kvg_ref, kg_ref, kb_ref,
                 qa_o, ckv_o, ckvt_o, qi_o, kk_o, wi_o, qb_o, kb_o, vb_o, xn_ref):
    xn_ref[...] = _rms(h_ref[0], g_ref[...]).astype(jnp.bfloat16)

    def seg(lo, width):
        return jnp.dot(xn_ref[...], w_ref[:, lo:lo + width], preferred_element_type=jnp.float32)

    qa_o[0] = seg(OFF_QA, W_QA).astype(jnp.bfloat16)

    c = _rms(seg(OFF_CKV, KV_RANK), kvg_ref[...])
    ckv_o[0] = c.astype(jnp.bfloat16)
    ct = c.T
    for t in range(ckvt_o.shape[1]):
        ckvt_o[0, t] = ct[:, t * BLOCK:(t + 1) * BLOCK].astype(jnp.bfloat16)

    qi_o[0] = seg(OFF_QI, W_QI).astype(jnp.bfloat16)

    k = seg(OFF_KK, 2 * IDX_DIM)
    mu = jnp.mean(k, axis=-1, keepdims=True)
    kc = k - mu
    var = jnp.mean(kc * kc, axis=-1, keepdims=True)
    kk_o[0] = (kc * lax.rsqrt(var + EPS) * kg_ref[...] + kb_ref[...]).astype(jnp.bfloat16)

    wi_o[0] = seg(OFF_WI, LANES) * (IDX_HEADS ** -0.5 * IDX_DIM ** -0.5)
    qb_o[0] = (seg(OFF_QB, W_QB) * (HD_B ** -0.5)).astype(jnp.bfloat16)
    kb_o[0] = seg(OFF_KB, 2 * W_KVB).astype(jnp.bfloat16)
    vb_o[0] = seg(OFF_VB, 2 * W_KVB).astype(jnp.bfloat16)


def _proj(h, g, w, kvg, kg, kb):
    b, s, d = h.shape
    tm = min(PROJ_TOKEN_TILE, s)
    bf = jnp.bfloat16

    def tok(width, dtype):
        return (jax.ShapeDtypeStruct((b, s, width), dtype),
                pl.BlockSpec((1, tm, width), lambda bi, ti: (bi, ti, 0)))

    outs = [
        tok(W_QA, bf),
        tok(KV_RANK, bf),
        (jax.ShapeDtypeStruct((b, s // BLOCK, KV_RANK, BLOCK), bf),
         pl.BlockSpec((1, tm // BLOCK, KV_RANK, BLOCK), lambda bi, ti: (bi, ti, 0, 0))),
        tok(W_QI, bf),
        tok(2 * IDX_DIM, bf),
        tok(LANES, jnp.float32),
        tok(W_QB, bf),
        tok(2 * W_KVB, bf),
        tok(2 * W_KVB, bf),
    ]
    const = lambda bi, ti: (0, 0)
    return pl.pallas_call(
        _proj_kernel,
        grid=(b, s // tm),
        in_specs=[
            pl.BlockSpec((1, tm, d), lambda bi, ti: (bi, ti, 0)),
            pl.BlockSpec((1, d), const),
            pl.BlockSpec((d, D_IN_PAD), const, pipeline_mode=pl.Buffered(1)),
            pl.BlockSpec((1, KV_RANK), const),
            pl.BlockSpec((1, 2 * IDX_DIM), const),
            pl.BlockSpec((1, 2 * IDX_DIM), const),
        ],
        out_specs=[o[1] for o in outs],
        out_shape=[o[0] for o in outs],
        scratch_shapes=[pltpu.VMEM((tm, d), bf)],
        compiler_params=pltpu.CompilerParams(
            dimension_semantics=("parallel", "parallel"),
            vmem_limit_bytes=VMEM_LIMIT_BYTES),
        name="proj",
    )(h, g, w, kvg, kg, kb)


def _sortable_to_f32(u):
    k = u ^ jnp.int32(-2 ** 31)
    bits = k ^ ((k >> 31) & jnp.int32(0x7FFFFFFF))
    return lax.bitcast_convert_type(bits, jnp.float32)


def _dsa_kernel(qa_ref, qi_ref, wi_ref, kk_ref, ckv_ref, ckvt_ref, wuk_ref, wuv_ref, bias_ref,
                o_ref, score_ref, acc_ref, m_ref, l_ref, qlat_ref, *chain_refs,
                k_sel, tk, idx_bits, scan_tiles):
    n_chains = H_A // ATTN_HEADS_PER_CHAIN
    chain_cols = ATTN_HEADS_PER_CHAIN * BLOCK
    x_refs, p_refs, alpha_refs = (chain_refs[k * n_chains:(k + 1) * n_chains] for k in range(3))
    i = pl.program_id(1)
    blocks_per_tile = tk // BLOCK
    n_tiles = i // blocks_per_tile + 1
    f32 = jnp.float32
    bf16 = jnp.bfloat16

    lane = lax.broadcasted_iota(jnp.int32, (BLOCK, LANES), 1)
    parts = []
    for h in range(IDX_HEADS):
        pair = qi_ref[0, :, (h // 2) * LANES:(h // 2 + 1) * LANES].astype(f32)
        keep = (lane < IDX_DIM) if h % 2 == 0 else (lane >= IDX_DIM)
        parts.append(jnp.where(keep, pair, 0.0).astype(bf16))
    qm = jnp.concatenate(parts, axis=0)
    wit = wi_ref[0].T

    t_pos = i * BLOCK + lax.broadcasted_iota(jnp.int32, (tk, LANES), 1)
    row = lax.broadcasted_iota(jnp.int32, (tk, LANES), 0)

    def score_body(kt, carry):
        base = pl.multiple_of(kt * tk, tk)
        keys = kk_ref[0, pl.ds(base, tk), :]
        sc = None
        for g in range(0, IDX_HEADS, SCORE_HEADS_PER_DOT):
            d = lax.dot_general(keys, qm[g * BLOCK:(g + SCORE_HEADS_PER_DOT) * BLOCK],
                                (((1,), (1,)), ((), ())), preferred_element_type=f32)
            for a in range(SCORE_HEADS_PER_DOT):
                term = jnp.maximum(d[:, a * BLOCK:(a + 1) * BLOCK], 0.0) * wit[g + a:g + a + 1, :]
                sc = term if sc is None else sc + term
        score_ref[pl.ds(base, tk), :] = jnp.where(base + row <= t_pos, sc, -jnp.inf)
        return carry

    lax.fori_loop(0, n_tiles, score_body, 0)

    n_scan = (n_tiles + scan_tiles - 1) // scan_tiles
    scan_rows = scan_tiles * tk

    def pad_body(kt, carry):
        score_ref[pl.ds(pl.multiple_of(kt * tk, tk), tk), :] = jnp.full((tk, LANES), -jnp.inf, f32)
        return carry

    lax.fori_loop(n_tiles, n_scan * scan_tiles, pad_body, 0)

    acc_rows = COUNT_ACCUMULATORS * SUBLANES
    scan_row = lax.broadcasted_iota(jnp.int32, (scan_rows, LANES), 0)

    def count(pred):
        def body(c, acc):
            base = pl.multiple_of(c * scan_rows, scan_rows)
            hit = jnp.where(pred(score_ref[pl.ds(base, scan_rows), :], base), 1, 0)
            return acc + hit.reshape(scan_rows // acc_rows, acc_rows, LANES).sum(axis=0)
        acc = lax.fori_loop(0, n_scan, body, jnp.zeros((acc_rows, LANES), jnp.int32))
        return jnp.sum(acc, axis=0, keepdims=True)

    def bit_body(it, u):
        cand = u | lax.shift_left(jnp.int32(1), 31 - it)
        thr_c = _sortable_to_f32(cand)
        cnt = count(lambda blk, base: blk >= thr_c)
        return jnp.where(cnt >= k_sel, cand, u)

    u = lax.fori_loop(0, 32, bit_body, jnp.zeros((1, LANES), jnp.int32))
    thr = _sortable_to_f32(u)
    thr = jnp.where(thr >= F32_LOWEST, thr, F32_LOWEST)

    cnt_ge = count(lambda blk, base: blk >= thr)
    has_tie = jnp.max(jnp.where(cnt_ge > k_sel, 1.0, 0.0)) > 0.5

    @pl.when(has_tie)
    def _():
        need = k_sel - count(lambda blk, base: blk > thr)

        def idx_body(it, p):
            cand = p | lax.shift_left(jnp.int32(1), idx_bits - 1 - it)
            cnt = count(lambda blk, base: (blk == thr) & (base + scan_row < cand))
            return jnp.where(cnt < need, cand, p)

        last = lax.fori_loop(0, idx_bits, idx_body, jnp.zeros((1, LANES), jnp.int32))

        def drop_body(c, carry):
            base = pl.multiple_of(c * scan_rows, scan_rows)
            blk = score_ref[pl.ds(base, scan_rows), :]
            score_ref[pl.ds(base, scan_rows), :] = jnp.where(
                (blk == thr) & (base + scan_row > last), -jnp.inf, blk)
            return carry

        lax.fori_loop(0, n_scan, drop_body, 0)

    def mask_body(c, carry):
        rows = pl.ds(pl.multiple_of(c * scan_rows, scan_rows), scan_rows)
        score_ref[rows, :] = jnp.where(score_ref[rows, :] >= thr, 0.0, NEG)
        return carry

    lax.fori_loop(0, n_scan, mask_body, 0)

    scale = HD_A ** -0.5 * LOG2E
    for h in range(H_A):
        hs = slice(h * HD_A, (h + 1) * HD_A)
        qlt = lax.dot_general(wuk_ref[:, hs], qa_ref[0, :, hs], (((1,), (1,)), ((), ())),
                              preferred_element_type=f32)
        g, a = divmod(h, ATTN_HEADS_PER_CHAIN)
        qlat_ref[g, :, a * BLOCK:(a + 1) * BLOCK] = (qlt * scale).astype(bf16)

    def tile_mask(rows):
        return jnp.concatenate([score_ref[rows, :]] * ATTN_HEADS_PER_CHAIN, axis=1)

    near_blk = jnp.maximum(i - 1, 0)
    near_base = pl.multiple_of(near_blk * BLOCK, BLOCK)
    bias_off = pl.multiple_of(jnp.where(i > 0, 0, BLOCK), BLOCK)
    near_rows = pl.ds(near_base, 2 * BLOCK)
    near_c = ckv_ref[0, near_rows, :]
    near_ct = jnp.concatenate([ckvt_ref[0, near_blk], ckvt_ref[0, near_blk + 1]], axis=1)
    near_mask = tile_mask(near_rows)
    for g in range(n_chains):
        x = (jnp.dot(near_c, qlat_ref[g], preferred_element_type=f32)
             + bias_ref[pl.ds(bias_off, 2 * BLOCK), g * chain_cols:(g + 1) * chain_cols] + near_mask)
        m_new = jnp.max(x, axis=0, keepdims=True)
        p = jnp.exp2(x - m_new)
        m_ref[g] = m_new
        l_ref[g] = jnp.sum(p, axis=0, keepdims=True)
        acc_ref[g] = jnp.dot(near_ct, p.astype(bf16), preferred_element_type=f32)
    score_ref[near_rows, :] = jnp.full((2 * BLOCK, LANES), NEG, f32)

    n_far = (i - 1 + blocks_per_tile - 1) // blocks_per_tile

    def logits_stage(kt, g):
        kt = jnp.minimum(kt, n_far - 1)
        rows = pl.ds(pl.multiple_of(kt * tk, tk), tk)
        x_refs[g][...] = jnp.dot(ckv_ref[0, rows, :], qlat_ref[g],
                                 preferred_element_type=f32) + tile_mask(rows)

    def softmax_stage(g):
        x = x_refs[g][...]
        m_old = m_ref[g]
        m_new = jnp.maximum(m_old, jnp.max(x, axis=0, keepdims=True))
        p = jnp.exp2(x - m_new)
        alpha = jnp.exp2(m_old - m_new)
        l_ref[g] = alpha * l_ref[g] + jnp.sum(p, axis=0, keepdims=True)
        m_ref[g] = m_new
        alpha_refs[g][...] = alpha
        p_refs[g][...] = p.astype(bf16)

    def values_stage(kt, g):
        kt = jnp.maximum(kt, 0)
        c_t = jnp.concatenate(
            [ckvt_ref[0, kt * blocks_per_tile + j] for j in range(blocks_per_tile)], axis=1)
        acc_ref[g] = alpha_refs[g][...] * acc_ref[g] + jnp.dot(
            c_t, p_refs[g][...], preferred_element_type=f32)

    @pl.when(n_far > 0)
    def _():
        p_refs[n_chains - 1][...] = jnp.zeros((tk, chain_cols), bf16)
        alpha_refs[n_chains - 1][...] = jnp.ones((1, chain_cols), f32)
        for g in range(PIPELINE_LOOKAHEAD):
            logits_stage(0, g)

        def tile_body(kt, carry):
            for g in range(n_chains):
                ahead, behind = g + PIPELINE_LOOKAHEAD, g - 1
                logits_stage(kt + ahead // n_chains, ahead % n_chains)
                softmax_stage(g)
                values_stage(kt + behind // n_chains, behind % n_chains)
            return carry

        lax.fori_loop(0, n_far, tile_body, 0)
        values_stage(n_far - 1, n_chains - 1)

    for h in range(H_A):
        g, a = divmod(h, ATTN_HEADS_PER_CHAIN)
        hs = slice(a * BLOCK, (a + 1) * BLOCK)
        o_lat = (acc_ref[g, :, hs] / l_ref[g, :, hs]).T.astype(bf16)
        o_ref[0, :, h * HD_A:(h + 1) * HD_A] = jnp.dot(
            o_lat, wuv_ref[:, h * HD_A:(h + 1) * HD_A],
            preferred_element_type=f32).astype(bf16)


def _dsa(qa, qi, wi, kk, ckv, ckvt, wuk, wuv, bias, tk):
    b, s, _ = qa.shape
    nb = s // BLOCK
    assert nb >= 2
    k_sel = min(TOPK_MAX, s // 4)
    idx_bits = max(1, int(math.ceil(math.log2(s))))
    chain_cols = ATTN_HEADS_PER_CHAIN * BLOCK
    n_chains = H_A // ATTN_HEADS_PER_CHAIN
    scan_tiles = SCAN_TILES if (s // tk) % SCAN_TILES == 0 else 1
    blk = lambda width: pl.BlockSpec((1, BLOCK, width), lambda bi, i: (bi, i, 0))
    full = lambda width: pl.BlockSpec((1, s, width), lambda bi, i: (bi, 0, 0))
    const = lambda bi, i: (0, 0)
    return pl.pallas_call(
        functools.partial(_dsa_kernel, k_sel=k_sel, tk=tk, idx_bits=idx_bits, scan_tiles=scan_tiles),
        grid=(b, nb),
        in_specs=[
            blk(W_QA), blk(W_QI), blk(LANES),
            full(2 * IDX_DIM), full(KV_RANK),
            pl.BlockSpec((1, nb, KV_RANK, BLOCK), lambda bi, i: (bi, 0, 0, 0)),
            pl.BlockSpec((KV_RANK, W_QA), const),
            pl.BlockSpec((KV_RANK, W_QA), const),
            pl.BlockSpec(bias.shape, const),
        ],
        out_specs=blk(W_QA),
        out_shape=jax.ShapeDtypeStruct((b, s, W_QA), jnp.bfloat16),
        scratch_shapes=[
            pltpu.VMEM((s, LANES), jnp.float32),
            pltpu.VMEM((n_chains, KV_RANK, chain_cols), jnp.float32),
            pltpu.VMEM((n_chains, 1, chain_cols), jnp.float32),
            pltpu.VMEM((n_chains, 1, chain_cols), jnp.float32),
            pltpu.VMEM((n_chains, KV_RANK, chain_cols), jnp.bfloat16),
        ]
          + [pltpu.VMEM((tk, chain_cols), jnp.float32)] * n_chains
          + [pltpu.VMEM((tk, chain_cols), jnp.bfloat16)] * n_chains
          + [pltpu.VMEM((1, chain_cols), jnp.float32)] * n_chains,
        compiler_params=pltpu.CompilerParams(
            dimension_semantics=("parallel", "arbitrary"),
            vmem_limit_bytes=VMEM_LIMIT_BYTES),
        name="dsa",
    )(qa, qi, wi, kk, ckv, ckvt, wuk, wuv, bias)


def _swa_kernel(sink_ref, q_ref, kp_ref, kc_ref, vp_ref, vc_ref, bias_ref, o_ref):
    i = pl.program_id(1)
    f32 = jnp.float32
    bf16 = jnp.bfloat16
    lane = lax.broadcasted_iota(jnp.int32, (BLOCK, LANES), 1)
    col = lax.broadcasted_iota(jnp.int32, (BLOCK, 2 * BLOCK), 1)
    prev_ok = (col >= BLOCK) | (i > 0)
    k_all = jnp.concatenate([kp_ref[0], kc_ref[0]], axis=0)
    v_all = jnp.concatenate([vp_ref[0], vc_ref[0]], axis=0)
    for pair in range(H_B // 2):
        kvh = (2 * pair) // GQA_GROUP
        kd = k_all[:, kvh * LANES:(kvh + 1) * LANES]
        vd = v_all[:, kvh * LANES:(kvh + 1) * LANES]
        q2 = q_ref[0, :, pair * LANES:(pair + 1) * LANES].astype(f32)
        halves = []
        for sub in range(2):
            h = 2 * pair + sub
            keep = (lane < HD_B) if sub == 0 else (lane >= HD_B)
            qm = jnp.where(keep, q2, 0.0).astype(bf16)
            lg = lax.dot_general(qm, kd, (((1,), (1,)), ((), ())),
                                 preferred_element_type=f32) + bias_ref[h]
            lg = jnp.where(prev_ok, lg, NEG)
            sink = sink_ref[h]
            m = jnp.maximum(jnp.max(lg, axis=-1, keepdims=True), sink)
            p = jnp.exp(lg - m)
            denom = jnp.sum(p, axis=-1, keepdims=True) + jnp.exp(sink - m)
            halves.append(jnp.dot(p.astype(bf16), vd, preferred_element_type=f32) / denom)
        o_ref[0, :, pair * LANES:(pair + 1) * LANES] = jnp.where(
            lane < HD_B, halves[0], halves[1]).astype(bf16)


def _swa(sinks, qb, kb, vb, bias):
    b, s, _ = qb.shape
    nb = s // BLOCK
    cur = lambda width: pl.BlockSpec((1, BLOCK, width), lambda bi, i: (bi, i, 0))
    prev = lambda width: pl.BlockSpec((1, BLOCK, width), lambda bi, i: (bi, jnp.maximum(i - 1, 0), 0))
    return pl.pallas_call(
        _swa_kernel,
        grid=(b, nb),
        in_specs=[
            pl.BlockSpec(memory_space=pltpu.SMEM),
            cur(W_QB), prev(2 * W_KVB), cur(2 * W_KVB), prev(2 * W_KVB), cur(2 * W_KVB),
            pl.BlockSpec(bias.shape, lambda bi, i: (0, 0, 0)),
        ],
        out_specs=cur(W_QB),
        out_shape=jax.ShapeDtypeStruct((b, s, W_QB), jnp.bfloat16),
        compiler_params=pltpu.CompilerParams(
            dimension_semantics=("parallel", "parallel"),
            vmem_limit_bytes=VMEM_LIMIT_BYTES),
        name="swa",
    )(sinks, qb, kb, kb, vb, vb, bias)


def _out_kernel(h_ref, oa_ref, ob_ref, wa_ref, wb_ref, o_ref):
    o_ref[...] = (h_ref[...]
                  + jnp.dot(oa_ref[...], wa_ref[...], preferred_element_type=jnp.float32)
                  + jnp.dot(ob_ref[...], wb_ref[...], preferred_element_type=jnp.float32))


def _out_proj(h, oa, ob, w):
    n, d = h.shape
    tm = min(OUT_TOKEN_TILE, n)
    wa_rows, wb_rows = oa.shape[1], ob.shape[1]
    assert wa_rows == wb_rows and w.shape[0] == wa_rows + wb_rows
    return pl.pallas_call(
        _out_kernel,
        grid=(n // tm,),
        in_specs=[
            pl.BlockSpec((tm, d), lambda i: (i, 0)),
            pl.BlockSpec((tm, wa_rows), lambda i: (i, 0)),
            pl.BlockSpec((tm, wb_rows), lambda i: (i, 0)),
            pl.BlockSpec((wa_rows, d), lambda i: (0, 0), pipeline_mode=pl.Buffered(1)),
            pl.BlockSpec((wb_rows, d), lambda i: (1, 0), pipeline_mode=pl.Buffered(1)),
        ],
        out_specs=pl.BlockSpec((tm, d), lambda i: (i, 0)),
        out_shape=jax.ShapeDtypeStruct((n, d), jnp.float32),
        compiler_params=pltpu.CompilerParams(
            dimension_semantics=("parallel",),
            vmem_limit_bytes=VMEM_LIMIT_BYTES),
        name="out_proj",
    )(h, oa, ob, w, w)


def _t5_bucket_np(dist):
    n = np.maximum(dist, 0)
    max_exact = N_BUCKETS // 2
    nf = np.maximum(n, 1).astype(np.float32)
    large = max_exact + (np.log(nf / np.float32(max_exact)) / np.float32(math.log(MAX_DISTANCE / max_exact))
                         * np.float32(N_BUCKETS - max_exact)).astype(np.int32)
    large = np.minimum(large, N_BUCKETS - 1)
    return np.where(n < max_exact, n, large)


def _band_distance():
    q = np.arange(BLOCK)[:, None]
    kj = np.arange(2 * BLOCK)[None, :]
    return BLOCK + q - kj


def _band_bias_kernel(table_ref, bucket_ref, o_ref):
    h = pl.program_id(0)
    bucket = bucket_ref[...]
    out = jnp.zeros(bucket.shape, jnp.float32)
    for b in range(N_BUCKETS):
        out = jnp.where(bucket == b, table_ref[b, h], out)
    o_ref[0] = out


def _band_bias(rel_bias):
    n_heads = rel_bias.shape[1]
    bucket = jnp.asarray(_t5_bucket_np(_band_distance()).astype(np.int32))
    return pl.pallas_call(
        _band_bias_kernel,
        grid=(n_heads,),
        in_specs=[
            pl.BlockSpec(memory_space=pltpu.SMEM),
            pl.BlockSpec(bucket.shape, lambda h: (0, 0)),
        ],
        out_specs=pl.BlockSpec((1,) + bucket.shape, lambda h: (h, 0, 0)),
        out_shape=jax.ShapeDtypeStruct((n_heads,) + bucket.shape, jnp.float32),
        compiler_params=pltpu.CompilerParams(dimension_semantics=("parallel",)),
        name="band_bias",
    )(rel_bias, bucket)


def _layout_w_in(w):
    splits = np.cumsum([W_QA, KV_RANK, W_QI, IDX_DIM, IDX_HEADS, W_QB, W_KVB])
    qa, ckv, qi, ki, wi, qb, kb, vb = jnp.split(w, [int(c) for c in splits], axis=-1)
    d = w.shape[0]
    dup = lambda a: jnp.concatenate(
        [a[:, kv * HD_B:(kv + 1) * HD_B] for kv in range(KVH_B) for _ in range(2)], axis=-1)
    wi_pad = jnp.concatenate([wi, jnp.zeros((d, LANES - IDX_HEADS), w.dtype)], axis=-1)
    out = jnp.concatenate([qa, ckv, qi, ki, ki, wi_pad, qb, dup(kb), dup(vb)], axis=-1)
    assert out.shape[1] == D_IN_PAD
    return out.astype(jnp.bfloat16)


def _cast_kernel(w_ref, o_ref):
    o_ref[...] = w_ref[...].astype(jnp.bfloat16)


def _cast_bf16(w):
    r, c = w.shape
    rows = min(CAST_ROW_TILE * 2, r)
    return pl.pallas_call(
        _cast_kernel,
        grid=(r // rows,),
        in_specs=[pl.BlockSpec((rows, c), lambda i: (i, 0))],
        out_specs=pl.BlockSpec((rows, c), lambda i: (i, 0)),
        out_shape=jax.ShapeDtypeStruct((r, c), jnp.bfloat16),
        compiler_params=pltpu.CompilerParams(
            dimension_semantics=("parallel",), vmem_limit_bytes=VMEM_LIMIT_BYTES),
        name="cast_bf16",
    )(w)


def _tile_major_kernel(w_ref, o_ref):
    tf = o_ref.shape[2]
    for j in range(o_ref.shape[0]):
        o_ref[j] = w_ref[:, j * tf:(j + 1) * tf].astype(jnp.bfloat16)


def _tile_major(w):
    d, f = w.shape
    tf = FFN_HIDDEN_TILE
    rows = CAST_ROW_TILE
    return pl.pallas_call(
        _tile_major_kernel,
        grid=(d // rows,),
        in_specs=[pl.BlockSpec((rows, f), lambda i: (i, 0))],
        out_specs=pl.BlockSpec((f // tf, rows, tf), lambda i: (0, i, 0)),
        out_shape=jax.ShapeDtypeStruct((f // tf, d, tf), jnp.bfloat16),
        compiler_params=pltpu.CompilerParams(
            dimension_semantics=("parallel",), vmem_limit_bytes=VMEM_LIMIT_BYTES),
        name="tile_major",
    )(w)


def kernel(x, rel_bias, ffn1_norm, ffn1_gate, ffn1_up, ffn1_down, mix_norm, w_in, kv_norm,
           idx_k_norm_g, idx_k_norm_b, w_uk, w_uv, sinks, w_out, ffn2_norm, ffn2_gate,
           ffn2_up, ffn2_down, final_norm):
    b, s, d = x.shape
    depth = w_in.shape[0]
    bf = jnp.bfloat16
    tk = min(DSA_KEY_TILE, s)
    assert s % BLOCK == 0 and s % tk == 0

    band = _band_bias(rel_bias)
    dist = _band_distance()
    rel = (band[:H_A] - rel_bias[N_BUCKETS - 1, :H_A][:, None, None]) * LOG2E
    rel = jnp.where(jnp.asarray(dist >= 0)[None], rel, 0.0)
    rel = jnp.transpose(rel, (2, 0, 1)).reshape(2 * BLOCK, H_A * BLOCK)
    dsa_bias = jnp.concatenate([rel, jnp.zeros((BLOCK, H_A * BLOCK), jnp.float32)], axis=0)
    swa_bias = jnp.where(jnp.asarray((dist >= 0) & (dist < WINDOW))[None], band[H_A:], NEG)

    fin = final_norm.reshape(1, d)
    dup2 = lambda a: jnp.concatenate([a, a], axis=-1).reshape(1, 2 * IDX_DIM)

    h = x.reshape(b * s, d)
    for l in range(depth):
        h = _ffn(h, ffn1_norm[l].reshape(1, d), _tile_major(ffn1_gate[l]), _tile_major(ffn1_up[l]),
                 _cast_bf16(ffn1_down[l]), fin, False)
        qa, ckv, ckvt, qi, kk, wi, qb, kb, vb = _proj(
            h.reshape(b, s, d), mix_norm[l].reshape(1, d), _layout_w_in(w_in[l]),
            kv_norm[l].reshape(1, KV_RANK), dup2(idx_k_norm_g[l]), dup2(idx_k_norm_b[l]))
        o_a = _dsa(qa, qi, wi, kk, ckv, ckvt,
                   w_uk[l].reshape(KV_RANK, W_QA).astype(bf),
                   w_uv[l].reshape(KV_RANK, W_QA).astype(bf), dsa_bias, tk)
        o_b = _swa(sinks[l], qb, kb, vb, swa_bias)
        h = _out_proj(h, o_a.reshape(b * s, W_QA), o_b.reshape(b * s, W_QB), _cast_bf16(w_out[l]))
        h = _ffn(h, ffn2_norm[l].reshape(1, d), _tile_major(ffn2_gate[l]), _tile_major(ffn2_up[l]),
                 _cast_bf16(ffn2_down[l]), fin, l == depth - 1)
    return h.reshape(b, s, d)
```

```python
import functools
import math

import numpy as np
import jax
import jax.numpy as jnp
from jax import lax
from jax.experimental import pallas as pl
from jax.experimental.pallas import tpu as pltpu

H_A = 8
HD_A = 128
KV_RANK = 256
IDX_HEADS = 8
IDX_DIM = 64
TOPK_MAX = 256
H_B = 16
KVH_B = 2
GQA_GROUP = H_B // KVH_B
HD_B = 64
WINDOW = 128
BLOCK = 128
N_BUCKETS = 32
MAX_DISTANCE = 128
EPS = 1e-6

LANES = 128
SUBLANES = 8
VMEM_LIMIT_BYTES = 56 * 1024 * 1024

FFN_TOKEN_TILE = 1024
FFN_HIDDEN_TILE = 512
PROJ_TOKEN_TILE = 512
OUT_TOKEN_TILE = 512
CAST_ROW_TILE = 256
DSA_KEY_TILE = 512
COUNT_ACCUMULATORS = 4
ATTN_HEADS_PER_CHAIN = 2
SCORE_HEADS_PER_DOT = 2
SCAN_TILES = 2
PIPELINE_LOOKAHEAD = 2

NEG = -1e30
F32_LOWEST = float(np.finfo(np.float32).min)
LOG2E = math.log2(math.e)

W_QA = H_A * HD_A
W_QI = IDX_HEADS * IDX_DIM
W_QB = H_B * HD_B
W_KVB = KVH_B * HD_B
OFF_QA = 0
OFF_CKV = OFF_QA + W_QA
OFF_QI = OFF_CKV + KV_RANK
OFF_KK = OFF_QI + W_QI
OFF_WI = OFF_KK + 2 * IDX_DIM
OFF_QB = OFF_WI + LANES
OFF_KB = OFF_QB + W_QB
OFF_VB = OFF_KB + 2 * W_KVB
D_IN_PAD = OFF_VB + 2 * W_KVB


def _rms(x, g):
    return x * lax.rsqrt(jnp.mean(x * x, axis=-1, keepdims=True) + EPS) * g


def _ffn_kernel(h_ref, g_ref, wg_ref, wu_ref, wd_ref, fg_ref, o_ref, xn_ref, *, final_norm):
    j = pl.program_id(1)

    @pl.when(j == 0)
    def _():
        x = h_ref[...]
        xn_ref[...] = _rms(x, g_ref[...]).astype(jnp.bfloat16)
        o_ref[...] = x

    xn = xn_ref[...]
    gate = jnp.dot(xn, wg_ref[0], preferred_element_type=jnp.float32)
    up = jnp.dot(xn, wu_ref[0], preferred_element_type=jnp.float32)
    act = (gate * jax.nn.sigmoid(gate) * (0.5 * up)).astype(jnp.bfloat16)
    o_ref[...] += jnp.dot(act, wd_ref[...], preferred_element_type=jnp.float32)

    if final_norm:
        @pl.when(j == pl.num_programs(1) - 1)
        def _():
            o_ref[...] = _rms(o_ref[...], fg_ref[...])


def _ffn(h, g, wg, wu, wd, fg, final_norm):
    n, d = h.shape
    tf = FFN_HIDDEN_TILE
    f = wg.shape[0] * tf
    assert wg.shape == wu.shape == (f // tf, d, tf) and wd.shape == (f, d)
    tm = min(FFN_TOKEN_TILE, n)
    return pl.pallas_call(
        functools.partial(_ffn_kernel, final_norm=final_norm),
        grid=(n // tm, f // tf),
        in_specs=[
            pl.BlockSpec((tm, d), lambda i, j: (i, 0)),
            pl.BlockSpec((1, d), lambda i, j: (0, 0)),
            pl.BlockSpec((1, d, tf), lambda i, j: (j, 0, 0)),
            pl.BlockSpec((1, d, tf), lambda i, j: (j, 0, 0)),
            pl.BlockSpec((tf, d), lambda i, j: (j, 0)),
            pl.BlockSpec((1, d), lambda i, j: (0, 0)),
        ],
        out_specs=pl.BlockSpec((tm, d), lambda i, j: (i, 0)),
        out_shape=jax.ShapeDtypeStruct((n, d), jnp.float32),
        scratch_shapes=[pltpu.VMEM((tm, d), jnp.bfloat16)],
        compiler_params=pltpu.CompilerParams(
            dimension_semantics=("parallel", "arbitrary"),
            vmem_limit_bytes=VMEM_LIMIT_BYTES),
        name="ffn",
    )(h, g, wg, wu, wd, fg)


def _proj_kernel(h_ref, g_ref, w_ref, kvg_ref, kg_ref, kb_ref,
                 qa_o, ckv_o, ckvt_o, qi_o, kk_o, wi_o, qb_o, kb_o, vb_o, xn_ref):
    xn_ref[...] = _rms(h_ref[0], g_ref[...]).astype(jnp.bfloat16)

    def seg(lo, width):
        return jnp.dot(xn_ref[...], w_ref[:, lo:lo + width], preferred_element_type=jnp.float32)

    qa_o[0] = seg(OFF_QA, W_QA).astype(jnp.bfloat16)

    c = _rms(seg(OFF_CKV, KV_RANK), kvg_ref[...])
    ckv_o[0] = c.astype(jnp.bfloat16)
    ct = c.T
    for t in range(ckvt_o.shape[1]):
        ckvt_o[0, t] = ct[:, t * BLOCK:(t + 1) * BLOCK].astype(jnp.bfloat16)

    qi_o[0] = seg(OFF_QI, W_QI).astype(jnp.bfloat16)

    k = seg(OFF_KK, 2 * IDX_DIM)
    mu = jnp.mean(k, axis=-1, keepdims=True)
    kc = k - mu
    var = jnp.mean(kc * kc, axis=-1, keepdims=True)
    kk_o[0] = (kc * lax.rsqrt(var + EPS) * kg_ref[...] + kb_ref[...]).astype(jnp.bfloat16)

    wi_o[0] = seg(OFF_WI, LANES) * (IDX_HEADS ** -0.5 * IDX_DIM ** -0.5)
    qb_o[0] = (seg(OFF_QB, W_QB) * (HD_B ** -0.5)).astype(jnp.bfloat16)
    kb_o[0] = seg(OFF_KB, 2 * W_KVB).astype(jnp.bfloat16)
    vb_o[0] = seg(OFF_VB, 2 * W_KVB).astype(jnp.bfloat16)


def _proj(h, g, w, kvg, kg, kb):
    b, s, d = h.shape
    tm = min(PROJ_TOKEN_TILE, s)
    bf = jnp.bfloat16

    def tok(width, dtype):
        return (jax.ShapeDtypeStruct((b, s, width), dtype),
                pl.BlockSpec((1, tm, width), lambda bi, ti: (bi, ti, 0)))

    outs = [
        tok(W_QA, bf),
        tok(KV_RANK, bf),
        (jax.ShapeDtypeStruct((b, s // BLOCK, KV_RANK, BLOCK), bf),
         pl.BlockSpec((1, tm // BLOCK, KV_RANK, BLOCK), lambda bi, ti: (bi, ti, 0, 0))),
        tok(W_QI, bf),
        tok(2 * IDX_DIM, bf),
        tok(LANES, jnp.float32),
        tok(W_QB, bf),
        tok(2 * W_KVB, bf),
        tok(2 * W_KVB, bf),
    ]
    const = lambda bi, ti: (0, 0)
    return pl.pallas_call(
        _proj_kernel,
        grid=(b, s // tm),
        in_specs=[
            pl.BlockSpec((1, tm, d), lambda bi, ti: (bi, ti, 0)),
            pl.BlockSpec((1, d), const),
            pl.BlockSpec((d, D_IN_PAD), const, pipeline_mode=pl.Buffered(1)),
            pl.BlockSpec((1, KV_RANK), const),
            pl.BlockSpec((1, 2 * IDX_DIM), const),
            pl.BlockSpec((1, 2 * IDX_DIM), const),
        ],
        out_specs=[o[1] for o in outs],
        out_shape=[o[0] for o in outs],
        scratch_shapes=[pltpu.VMEM((tm, d), bf)],
        compiler_params=pltpu.CompilerParams(
            dimension_semantics=("parallel", "parallel"),
            vmem_limit_bytes=VMEM_LIMIT_BYTES),
        name="proj",
    )(h, g, w, kvg, kg, kb)


def _sortable_to_f32(u):
    k = u ^ jnp.int32(-2 ** 31)
    bits = k ^ ((k >> 31) & jnp.int32(0x7FFFFFFF))
    return lax.bitcast_convert_type(bits, jnp.float32)


def _dsa_kernel(qa_ref, qi_ref, wi_ref, kk_ref, ckv_ref, ckvt_ref, wuk_ref, wuv_ref, bias_ref,
                o_ref, score_ref, acc_ref, m_ref, l_ref, qlat_ref, *chain_refs,
                k_sel, tk, idx_bits, scan_tiles):
    n_chains = H_A // ATTN_HEADS_PER_CHAIN
    chain_cols = ATTN_HEADS_PER_CHAIN * BLOCK
    x_refs, p_refs, alpha_refs = (chain_refs[k * n_chains:(k + 1) * n_chains] for k in range(3))
    i = pl.program_id(1)
    blocks_per_tile = tk // BLOCK
    n_tiles = i // blocks_per_tile + 1
    f32 = jnp.float32
    bf16 = jnp.bfloat16

    lane = lax.broadcasted_iota(jnp.int32, (BLOCK, LANES), 1)
    parts = []
    for h in range(IDX_HEADS):
        pair = qi_ref[0, :, (h // 2) * LANES:(h // 2 + 1) * LANES].astype(f32)
        keep = (lane < IDX_DIM) if h % 2 == 0 else (lane >= IDX_DIM)
        parts.append(jnp.where(keep, pair, 0.0).astype(bf16))
    qm = jnp.concatenate(parts, axis=0)
    wit = wi_ref[0].T

    t_pos = i * BLOCK + lax.broadcasted_iota(jnp.int32, (tk, LANES), 1)
    row = lax.broadcasted_iota(jnp.int32, (tk, LANES), 0)

    def score_body(kt, carry):
        base = pl.multiple_of(kt * tk, tk)
        keys = kk_ref[0, pl.ds(base, tk), :]
        sc = None
        for g in range(0, IDX_HEADS, SCORE_HEADS_PER_DOT):
            d = lax.dot_general(keys, qm[g * BLOCK:(g + SCORE_HEADS_PER_DOT) * BLOCK],
                                (((1,), (1,)), ((), ())), preferred_element_type=f32)
            for a in range(SCORE_HEADS_PER_DOT):
                term = jnp.maximum(d[:, a * BLOCK:(a + 1) * BLOCK], 0.0) * wit[g + a:g + a + 1, :]
                sc = term if sc is None else sc + term
        score_ref[pl.ds(base, tk), :] = jnp.where(base + row <= t_pos, sc, -jnp.inf)
        return carry

    lax.fori_loop(0, n_tiles, score_body, 0)

    n_scan = (n_tiles + scan_tiles - 1) // scan_tiles
    scan_rows = scan_tiles * tk

    def pad_body(kt, carry):
        score_ref[pl.ds(pl.multiple_of(kt * tk, tk), tk), :] = jnp.full((tk, LANES), -jnp.inf, f32)
        return carry

    lax.fori_loop(n_tiles, n_scan * scan_tiles, pad_body, 0)

    acc_rows = COUNT_ACCUMULATORS * SUBLANES
    scan_row = lax.broadcasted_iota(jnp.int32, (scan_rows, LANES), 0)

    def count(pred):
        def body(c, acc):
            base = pl.multiple_of(c * scan_rows, scan_rows)
            hit = jnp.where(pred(score_ref[pl.ds(base, scan_rows), :], base), 1, 0)
            return acc + hit.reshape(scan_rows // acc_rows, acc_rows, LANES).sum(axis=0)
        acc = lax.fori_loop(0, n_scan, body, jnp.zeros((acc_rows, LANES), jnp.int32))
        return jnp.sum(acc, axis=0, keepdims=True)

    def bit_body(it, carry):
        u, cnt_u = carry
        cand = u | lax.shift_left(jnp.int32(1), 31 - it)
        thr_c = _sortable_to_f32(cand)
        cnt = count(lambda blk, base: blk >= thr_c)
        take = cnt >= k_sel
        return jnp.where(take, cand, u), jnp.where(take, cnt, cnt_u)

    zero_row = jnp.zeros((1, LANES), jnp.int32)
    u, cnt_u = lax.fori_loop(0, 32, bit_body, (zero_row, zero_row))
    thr = _sortable_to_f32(u)
    enough = thr >= F32_LOWEST
    thr = jnp.where(enough, thr, F32_LOWEST)

    has_tie = jnp.max(jnp.where(enough & (cnt_u > k_sel), 1.0, 0.0)) > 0.5

    @pl.when(has_tie)
    def _():
        need = k_sel - count(lambda blk, base: blk > thr)

        def idx_body(it, p):
            cand = p | lax.shift_left(jnp.int32(1), idx_bits - 1 - it)
            cnt = count(lambda blk, base: (blk == thr) & (base + scan_row < cand))
            return jnp.where(cnt < need, cand, p)

        last = lax.fori_loop(0, idx_bits, idx_body, jnp.zeros((1, LANES), jnp.int32))

        def drop_body(c, carry):
            base = pl.multiple_of(c * scan_rows, scan_rows)
            blk = score_ref[pl.ds(base, scan_rows), :]
            score_ref[pl.ds(base, scan_rows), :] = jnp.where(
                (blk == thr) & (base + scan_row > last), -jnp.inf, blk)
            return carry

        lax.fori_loop(0, n_scan, drop_body, 0)

    def mask_body(c, carry):
        rows = pl.ds(pl.multiple_of(c * scan_rows, scan_rows), scan_rows)
        score_ref[rows, :] = jnp.where(score_ref[rows, :] >= thr, 0.0, NEG)
        return carry

    lax.fori_loop(0, n_scan, mask_body, 0)

    scale = HD_A ** -0.5 * LOG2E
    for h in range(H_A):
        hs = slice(h * HD_A, (h + 1) * HD_A)
        qlt = lax.dot_general(wuk_ref[:, hs], qa_ref[0, :, hs], (((1,), (1,)), ((), ())),
                              preferred_element_type=f32)
        g, a = divmod(h, ATTN_HEADS_PER_CHAIN)
        qlat_ref[g, :, a * BLOCK:(a + 1) * BLOCK] = (qlt * scale).astype(bf16)

    def tile_mask(rows):
        return jnp.concatenate([score_ref[rows, :]] * ATTN_HEADS_PER_CHAIN, axis=1)

    near_blk = jnp.maximum(i - 1, 0)
    near_base = pl.multiple_of(near_blk * BLOCK, BLOCK)
    bias_off = pl.multiple_of(jnp.where(i > 0, 0, BLOCK), BLOCK)
    near_rows = pl.ds(near_base, 2 * BLOCK)
    near_c = ckv_ref[0, near_rows, :]
    near_ct = jnp.concatenate([ckvt_ref[0, near_blk], ckvt_ref[0, near_blk + 1]], axis=1)
    near_mask = tile_mask(near_rows)
    for g in range(n_chains):
        x = (jnp.dot(near_c, qlat_ref[g], preferred_element_type=f32)
             + bias_ref[pl.ds(bias_off, 2 * BLOCK), g * chain_cols:(g + 1) * chain_cols] + near_mask)
        m_new = jnp.max(x, axis=0, keepdims=True)
        p = jnp.exp2(x - m_new)
        m_ref[g] = m_new
        l_ref[g] = jnp.sum(p, axis=0, keepdims=True)
        acc_ref[g] = jnp.dot(near_ct, p.astype(bf16), preferred_element_type=f32)
    score_ref[near_rows, :] = jnp.full((2 * BLOCK, LANES), NEG, f32)

    n_far = (i - 1 + blocks_per_tile - 1) // blocks_per_tile

    def logits_stage(kt, g):
        kt = jnp.minimum(kt, n_far - 1)
        rows = pl.ds(pl.multiple_of(kt * tk, tk), tk)
        x_refs[g][...] = jnp.dot(ckv_ref[0, rows, :], qlat_ref[g],
                                 preferred_element_type=f32) + tile_mask(rows)

    def softmax_stage(g):
        x = x_refs[g][...]
        m_old = m_ref[g]
        m_new = jnp.maximum(m_old, jnp.max(x, axis=0, keepdims=True))
        p = jnp.exp2(x - m_new)
        alpha = jnp.exp2(m_old - m_new)
        l_ref[g] = alpha * l_ref[g] + jnp.sum(p, axis=0, keepdims=True)
        m_ref[g] = m_new
        alpha_refs[g][...] = alpha
        p_refs[g][...] = p.astype(bf16)

    def values_stage(kt, g):
        kt = jnp.maximum(kt, 0)
        c_t = jnp.concatenate(
            [ckvt_ref[0, kt * blocks_per_tile + j] for j in range(blocks_per_tile)], axis=1)
        acc_ref[g] = alpha_refs[g][...] * acc_ref[g] + jnp.dot(
            c_t, p_refs[g][...], preferred_element_type=f32)

    @pl.when(n_far > 0)
    def _():
        p_refs[n_chains - 1][...] = jnp.zeros((tk, chain_cols), bf16)
        alpha_refs[n_chains - 1][...] = jnp.ones((1, chain_cols), f32)
        for g in range(PIPELINE_LOOKAHEAD):
            logits_stage(0, g)

        def tile_body(kt, carry):
            for g in range(n_chains):
                ahead, behind = g + PIPELINE_LOOKAHEAD, g - 1
                logits_stage(kt + ahead // n_chains, ahead % n_chains)
                softmax_stage(g)
                values_stage(kt + behind // n_chains, behind % n_chains)
            return carry

        lax.fori_loop(0, n_far, tile_body, 0)
        values_stage(n_far - 1, n_chains - 1)

    for h in range(H_A):
        g, a = divmod(h, ATTN_HEADS_PER_CHAIN)
        hs = slice(a * BLOCK, (a + 1) * BLOCK)
        o_lat = (acc_ref[g, :, hs] / l_ref[g, :, hs]).T.astype(bf16)
        o_ref[0, :, h * HD_A:(h + 1) * HD_A] = jnp.dot(
            o_lat, wuv_ref[:, h * HD_A:(h + 1) * HD_A],
            preferred_element_type=f32).astype(bf16)


def _dsa(qa, qi, wi, kk, ckv, ckvt, wuk, wuv, bias, tk):
    b, s, _ = qa.shape
    nb = s // BLOCK
    assert nb >= 2
    k_sel = min(TOPK_MAX, s // 4)
    idx_bits = max(1, int(math.ceil(math.log2(s))))
    chain_cols = ATTN_HEADS_PER_CHAIN * BLOCK
    n_chains = H_A // ATTN_HEADS_PER_CHAIN
    scan_tiles = SCAN_TILES if (s // tk) % SCAN_TILES == 0 else 1
    blk = lambda width: pl.BlockSpec((1, BLOCK, width), lambda bi, i: (bi, i, 0))
    full = lambda width: pl.BlockSpec((1, s, width), lambda bi, i: (bi, 0, 0))
    const = lambda bi, i: (0, 0)
    return pl.pallas_call(
        functools.partial(_dsa_kernel, k_sel=k_sel, tk=tk, idx_bits=idx_bits, scan_tiles=scan_tiles),
        grid=(b, nb),
        in_specs=[
            blk(W_QA), blk(W_QI), blk(LANES),
            full(2 * IDX_DIM), full(KV_RANK),
            pl.BlockSpec((1, nb, KV_RANK, BLOCK), lambda bi, i: (bi, 0, 0, 0)),
            pl.BlockSpec((KV_RANK, W_QA), const),
            pl.BlockSpec((KV_RANK, W_QA), const),
            pl.BlockSpec(bias.shape, const),
        ],
        out_specs=blk(W_QA),
        out_shape=jax.ShapeDtypeStruct((b, s, W_QA), jnp.bfloat16),
        scratch_shapes=[
            pltpu.VMEM((s, LANES), jnp.float32),
            pltpu.VMEM((n_chains, KV_RANK, chain_cols), jnp.float32),
            pltpu.VMEM((n_chains, 1, chain_cols), jnp.float32),
            pltpu.VMEM((n_chains, 1, chain_cols), jnp.float32),
            pltpu.VMEM((n_chains, KV_RANK, chain_cols), jnp.bfloat16),
        ]
          + [pltpu.VMEM((tk, chain_cols), jnp.float32)] * n_chains
          + [pltpu.VMEM((tk, chain_cols), jnp.bfloat16)] * n_chains
          + [pltpu.VMEM((1, chain_cols), jnp.float32)] * n_chains,
        compiler_params=pltpu.CompilerParams(
            dimension_semantics=("parallel", "arbitrary"),
            vmem_limit_bytes=VMEM_LIMIT_BYTES),
        name="dsa",
    )(qa, qi, wi, kk, ckv, ckvt, wuk, wuv, bias)


def _swa_kernel(sink_ref, q_ref, kp_ref, kc_ref, vp_ref, vc_ref, bias_ref, o_ref):
    i = pl.program_id(1)
    f32 = jnp.float32
    bf16 = jnp.bfloat16
    lane = lax.broadcasted_iota(jnp.int32, (BLOCK, LANES), 1)
    col = lax.broadcasted_iota(jnp.int32, (BLOCK, 2 * BLOCK), 1)
    prev_ok = (col >= BLOCK) | (i > 0)
    k_all = jnp.concatenate([kp_ref[0], kc_ref[0]], axis=0)
    v_all = jnp.concatenate([vp_ref[0], vc_ref[0]], axis=0)
    for pair in range(H_B // 2):
        kvh = (2 * pair) // GQA_GROUP
        kd = k_all[:, kvh * LANES:(kvh + 1) * LANES]
        vd = v_all[:, kvh * LANES:(kvh + 1) * LANES]
        q2 = q_ref[0, :, pair * LANES:(pair + 1) * LANES].astype(f32)
        halves = []
        for sub in range(2):
            h = 2 * pair + sub
            keep = (lane < HD_B) if sub == 0 else (lane >= HD_B)
            qm = jnp.where(keep, q2, 0.0).astype(bf16)
            lg = lax.dot_general(qm, kd, (((1,), (1,)), ((), ())),
                                 preferred_element_type=f32) + bias_ref[h]
            lg = jnp.where(prev_ok, lg, NEG)
            sink = sink_ref[h]
            m = jnp.maximum(jnp.max(lg, axis=-1, keepdims=True), sink)
            p = jnp.exp(lg - m)
            denom = jnp.sum(p, axis=-1, keepdims=True) + jnp.exp(sink - m)
            halves.append(jnp.dot(p.astype(bf16), vd, preferred_element_type=f32) / denom)
        o_ref[0, :, pair * LANES:(pair + 1) * LANES] = jnp.where(
            lane < HD_B, halves[0], halves[1]).astype(bf16)


def _swa(sinks, qb, kb, vb, bias):
    b, s, _ = qb.shape
    nb = s // BLOCK
    cur = lambda width: pl.BlockSpec((1, BLOCK, width), lambda bi, i: (bi, i, 0))
    prev = lambda width: pl.BlockSpec((1, BLOCK, width), lambda bi, i: (bi, jnp.maximum(i - 1, 0), 0))
    return pl.pallas_call(
        _swa_kernel,
        grid=(b, nb),
        in_specs=[
            pl.BlockSpec(memory_space=pltpu.SMEM),
            cur(W_QB), prev(2 * W_KVB), cur(2 * W_KVB), prev(2 * W_KVB), cur(2 * W_KVB),
            pl.BlockSpec(bias.shape, lambda bi, i: (0, 0, 0)),
        ],
        out_specs=cur(W_QB),
        out_shape=jax.ShapeDtypeStruct((b, s, W_QB), jnp.bfloat16),
        compiler_params=pltpu.CompilerParams(
            dimension_semantics=("parallel", "parallel"),
            vmem_limit_bytes=VMEM_LIMIT_BYTES),
        name="swa",
    )(sinks, qb, kb, kb, vb, vb, bias)


def _out_kernel(h_ref, oa_ref, ob_ref, wa_ref, wb_ref, o_ref):
    o_ref[...] = (h_ref[...]
                  + jnp.dot(oa_ref[...], wa_ref[...], preferred_element_type=jnp.float32)
                  + jnp.dot(ob_ref[...], wb_ref[...], preferred_element_type=jnp.float32))


def _out_proj(h, oa, ob, w):
    n, d = h.shape
    tm = min(OUT_TOKEN_TILE, n)
    wa_rows, wb_rows = oa.shape[1], ob.shape[1]
    assert wa_rows == wb_rows and w.shape[0] == wa_rows + wb_rows
    return pl.pallas_call(
        _out_kernel,
        grid=(n // tm,),
        in_specs=[
            pl.BlockSpec((tm, d), lambda i: (i, 0)),
            pl.BlockSpec((tm, wa_rows), lambda i: (i, 0)),
            pl.BlockSpec((tm, wb_rows), lambda i: (i, 0)),
            pl.BlockSpec((wa_rows, d), lambda i: (0, 0), pipeline_mode=pl.Buffered(1)),
            pl.BlockSpec((wb_rows, d), lambda i: (1, 0), pipeline_mode=pl.Buffered(1)),
        ],
        out_specs=pl.BlockSpec((tm, d), lambda i: (i, 0)),
        out_shape=jax.ShapeDtypeStruct((n, d), jnp.float32),
        compiler_params=pltpu.CompilerParams(
            dimension_semantics=("parallel",),
            vmem_limit_bytes=VMEM_LIMIT_BYTES),
        name="out_proj",
    )(h, oa, ob, w, w)


def _t5_bucket_np(dist):
    n = np.maximum(dist, 0)
    max_exact = N_BUCKETS // 2
    nf = np.maximum(n, 1).astype(np.float32)
    large = max_exact + (np.log(nf / np.float32(max_exact)) / np.float32(math.log(MAX_DISTANCE / max_exact))
                         * np.float32(N_BUCKETS - max_exact)).astype(np.int32)
    large = np.minimum(large, N_BUCKETS - 1)
    return np.where(n < max_exact, n, large)


def _band_distance():
    q = np.arange(BLOCK)[:, None]
    kj = np.arange(2 * BLOCK)[None, :]
    return BLOCK + q - kj


def _band_bias_kernel(table_ref, bucket_ref, o_ref):
    h = pl.program_id(0)
    bucket = bucket_ref[...]
    out = jnp.zeros(bucket.shape, jnp.float32)
    for b in range(N_BUCKETS):
        out = jnp.where(bucket == b, table_ref[b, h], out)
    o_ref[0] = out


def _band_bias(rel_bias):
    n_heads = rel_bias.shape[1]
    bucket = jnp.asarray(_t5_bucket_np(_band_distance()).astype(np.int32))
    return pl.pallas_call(
        _band_bias_kernel,
        grid=(n_heads,),
        in_specs=[
            pl.BlockSpec(memory_space=pltpu.SMEM),
            pl.BlockSpec(bucket.shape, lambda h: (0, 0)),
        ],
        out_specs=pl.BlockSpec((1,) + bucket.shape, lambda h: (h, 0, 0)),
        out_shape=jax.ShapeDtypeStruct((n_heads,) + bucket.shape, jnp.float32),
        compiler_params=pltpu.CompilerParams(dimension_semantics=("parallel",)),
        name="band_bias",
    )(rel_bias, bucket)


def _layout_w_in(w):
    splits = np.cumsum([W_QA, KV_RANK, W_QI, IDX_DIM, IDX_HEADS, W_QB, W_KVB])
    qa, ckv, qi, ki, wi, qb, kb, vb = jnp.split(w, [int(c) for c in splits], axis=-1)
    d = w.shape[0]
    dup = lambda a: jnp.concatenate(
        [a[:, kv * HD_B:(kv + 1) * HD_B] for kv in range(KVH_B) for _ in range(2)], axis=-1)
    wi_pad = jnp.concatenate([wi, jnp.zeros((d, LANES - IDX_HEADS), w.dtype)], axis=-1)
    out = jnp.concatenate([qa, ckv, qi, ki, ki, wi_pad, qb, dup(kb), dup(vb)], axis=-1)
    assert out.shape[1] == D_IN_PAD
    return out.astype(jnp.bfloat16)


def _cast_kernel(w_ref, o_ref):
    o_ref[...] = w_ref[0].astype(jnp.bfloat16)


def _cast_bf16(w, layer):
    _, r, c = w.shape
    rows = min(CAST_ROW_TILE * 2, r)
    return pl.pallas_call(
        _cast_kernel,
        grid=(r // rows,),
        in_specs=[pl.BlockSpec((1, rows, c), lambda i: (layer, i, 0))],
        out_specs=pl.BlockSpec((rows, c), lambda i: (i, 0)),
        out_shape=jax.ShapeDtypeStruct((r, c), jnp.bfloat16),
        compiler_params=pltpu.CompilerParams(
            dimension_semantics=("parallel",), vmem_limit_bytes=VMEM_LIMIT_BYTES),
        name="cast_bf16",
    )(w)


def _tile_major_kernel(w_ref, o_ref):
    tf = o_ref.shape[2]
    for j in range(o_ref.shape[0]):
        o_ref[j] = w_ref[0, :, j * tf:(j + 1) * tf].astype(jnp.bfloat16)


def _tile_major(w, layer):
    _, d, f = w.shape
    tf = FFN_HIDDEN_TILE
    rows = CAST_ROW_TILE
    return pl.pallas_call(
        _tile_major_kernel,
        grid=(d // rows,),
        in_specs=[pl.BlockSpec((1, rows, f), lambda i: (layer, i, 0))],
        out_specs=pl.BlockSpec((f // tf, rows, tf), lambda i: (0, i, 0)),
        out_shape=jax.ShapeDtypeStruct((f // tf, d, tf), jnp.bfloat16),
        compiler_params=pltpu.CompilerParams(
            dimension_semantics=("parallel",), vmem_limit_bytes=VMEM_LIMIT_BYTES),
        name="tile_major",
    )(w)


def kernel(x, rel_bias, ffn1_norm, ffn1_gate, ffn1_up, ffn1_down, mix_norm, w_in, kv_norm,
           idx_k_norm_g, idx_k_norm_b, w_uk, w_uv, sinks, w_out, ffn2_norm, ffn2_gate,
           ffn2_up, ffn2_down, final_norm):
    b, s, d = x.shape
    depth = w_in.shape[0]
    bf = jnp.bfloat16
    tk = min(DSA_KEY_TILE, s)
    assert s % BLOCK == 0 and s % tk == 0

    band = _band_bias(rel_bias)
    dist = _band_distance()
    rel = (band[:H_A] - rel_bias[N_BUCKETS - 1, :H_A][:, None, None]) * LOG2E
    rel = jnp.where(jnp.asarray(dist >= 0)[None], rel, 0.0)
    rel = jnp.transpose(rel, (2, 0, 1)).reshape(2 * BLOCK, H_A * BLOCK)
    dsa_bias = jnp.concatenate([rel, jnp.zeros((BLOCK, H_A * BLOCK), jnp.float32)], axis=0)
    swa_bias = jnp.where(jnp.asarray((dist >= 0) & (dist < WINDOW))[None], band[H_A:], NEG)

    fin = final_norm.reshape(1, d)
    dup2 = lambda a: jnp.concatenate([a, a], axis=-1).reshape(1, 2 * IDX_DIM)

    h = x.reshape(b * s, d)
    for l in range(depth):
        h = _ffn(h, ffn1_norm[l].reshape(1, d), _tile_major(ffn1_gate, l), _tile_major(ffn1_up, l),
                 _cast_bf16(ffn1_down, l), fin, False)
        qa, ckv, ckvt, qi, kk, wi, qb, kb, vb = _proj(
            h.reshape(b, s, d), mix_norm[l].reshape(1, d), _layout_w_in(w_in[l]),
            kv_norm[l].reshape(1, KV_RANK), dup2(idx_k_norm_g[l]), dup2(idx_k_norm_b[l]))
        o_a = _dsa(qa, qi, wi, kk, ckv, ckvt,
                   w_uk[l].reshape(KV_RANK, W_QA).astype(bf),
                   w_uv[l].reshape(KV_RANK, W_QA).astype(bf), dsa_bias, tk)
        o_b = _swa(sinks[l], qb, kb, vb, swa_bias)
        h = _out_proj(h, o_a.reshape(b * s, W_QA), o_b.reshape(b * s, W_QB), _cast_bf16(w_out, l))
        h = _ffn(h, ffn2_norm[l].reshape(1, d), _tile_major(ffn2_gate, l), _tile_major(ffn2_up, l),
                 _cast_bf16(ffn2_down, l), fin, l == depth - 1)
    return h.reshape(b, s, d)
```

```python
import functools
import math

import numpy as np
import jax
import jax.numpy as jnp
from jax import lax
from jax.experimental import pallas as pl
from jax.experimental.pallas import tpu as pltpu

H_A = 8
HD_A = 128
KV_RANK = 256
IDX_HEADS = 8
IDX_DIM = 64
TOPK_MAX = 256
H_B = 16
KVH_B = 2
GQA_GROUP = H_B // KVH_B
HD_B = 64
WINDOW = 128
BLOCK = 128
N_BUCKETS = 32
MAX_DISTANCE = 128
EPS = 1e-6

LANES = 128
SUBLANES = 8
VMEM_LIMIT_BYTES = 56 * 1024 * 1024

FFN_TOKEN_TILE = 1024
FFN_HIDDEN_TILE = 512
PROJ_TOKEN_TILE = 512
OUT_TOKEN_TILE = 512
CAST_ROW_TILE = 256
DSA_KEY_TILE = 512
COUNT_ACCUMULATORS = 4
ATTN_HEADS_PER_CHAIN = 2
SCORE_HEADS_PER_DOT = 2
SCAN_TILES = 2
BF16_ULP_BITS = 1 << 16
HI_DTYPE = jnp.bfloat16
PIPELINE_LOOKAHEAD = 2

NEG = -1e30
F32_LOWEST = float(np.finfo(np.float32).min)
LOG2E = math.log2(math.e)

W_QA = H_A * HD_A
W_QI = IDX_HEADS * IDX_DIM
W_QB = H_B * HD_B
W_KVB = KVH_B * HD_B
OFF_QA = 0
OFF_CKV = OFF_QA + W_QA
OFF_QI = OFF_CKV + KV_RANK
OFF_KK = OFF_QI + W_QI
OFF_WI = OFF_KK + 2 * IDX_DIM
OFF_QB = OFF_WI + LANES
OFF_KB = OFF_QB + W_QB
OFF_VB = OFF_KB + 2 * W_KVB
D_IN_PAD = OFF_VB + 2 * W_KVB


def _rms(x, g):
    return x * lax.rsqrt(jnp.mean(x * x, axis=-1, keepdims=True) + EPS) * g


def _ffn_kernel(h_ref, g_ref, wg_ref, wu_ref, wd_ref, fg_ref, o_ref, xn_ref, *, final_norm):
    j = pl.program_id(1)

    @pl.when(j == 0)
    def _():
        x = h_ref[...]
        xn_ref[...] = _rms(x, g_ref[...]).astype(jnp.bfloat16)
        o_ref[...] = x

    xn = xn_ref[...]
    gate = jnp.dot(xn, wg_ref[0], preferred_element_type=jnp.float32)
    up = jnp.dot(xn, wu_ref[0], preferred_element_type=jnp.float32)
    act = (gate * jax.nn.sigmoid(gate) * (0.5 * up)).astype(jnp.bfloat16)
    o_ref[...] += jnp.dot(act, wd_ref[...], preferred_element_type=jnp.float32)

    if final_norm:
        @pl.when(j == pl.num_programs(1) - 1)
        def _():
            o_ref[...] = _rms(o_ref[...], fg_ref[...])


def _ffn(h, g, wg, wu, wd, fg, final_norm):
    n, d = h.shape
    tf = FFN_HIDDEN_TILE
    f = wg.shape[0] * tf
    assert wg.shape == wu.shape == (f // tf, d, tf) and wd.shape == (f, d)
    tm = min(FFN_TOKEN_TILE, n)
    return pl.pallas_call(
        functools.partial(_ffn_kernel, final_norm=final_norm),
        grid=(n // tm, f // tf),
        in_specs=[
            pl.BlockSpec((tm, d), lambda i, j: (i, 0)),
            pl.BlockSpec((1, d), lambda i, j: (0, 0)),
            pl.BlockSpec((1, d, tf), lambda i, j: (j, 0, 0)),
            pl.BlockSpec((1, d, tf), lambda i, j: (j, 0, 0)),
            pl.BlockSpec((tf, d), lambda i, j: (j, 0)),
            pl.BlockSpec((1, d), lambda i, j: (0, 0)),
        ],
        out_specs=pl.BlockSpec((tm, d), lambda i, j: (i, 0)),
        out_shape=jax.ShapeDtypeStruct((n, d), jnp.float32),
        scratch_shapes=[pltpu.VMEM((tm, d), jnp.bfloat16)],
        compiler_params=pltpu.CompilerParams(
            dimension_semantics=("parallel", "arbitrary"),
            vmem_limit_bytes=VMEM_LIMIT_BYTES),
        name="ffn",
    )(h, g, wg, wu, wd, fg)


def _proj_kernel(h_ref, g_ref, w_ref, kvg_ref, kg_ref, kb_ref,
                 qa_o, ckv_o, ckvt_o, qi_o, kk_o, wi_o, qb_o, kb_o, vb_o, xn_ref):
    xn_ref[...] = _rms(h_ref[0], g_ref[...]).astype(jnp.bfloat16)

    def seg(lo, width):
        return jnp.dot(xn_ref[...], w_ref[:, lo:lo + width], preferred_element_type=jnp.float32)

    qa_o[0] = seg(OFF_QA, W_QA).astype(jnp.bfloat16)

    c = _rms(seg(OFF_CKV, KV_RANK), kvg_ref[...])
    ckv_o[0] = c.astype(jnp.bfloat16)
    ct = c.T
    for t in range(ckvt_o.shape[1]):
        ckvt_o[0, t] = ct[:, t * BLOCK:(t + 1) * BLOCK].astype(jnp.bfloat16)

    qi_o[0] = seg(OFF_QI, W_QI).astype(jnp.bfloat16)

    k = seg(OFF_KK, 2 * IDX_DIM)
    mu = jnp.mean(k, axis=-1, keepdims=True)
    kc = k - mu
    var = jnp.mean(kc * kc, axis=-1, keepdims=True)
    kk_o[0] = (kc * lax.rsqrt(var + EPS) * kg_ref[...] + kb_ref[...]).astype(jnp.bfloat16)

    wi_o[0] = seg(OFF_WI, LANES) * (IDX_HEADS ** -0.5 * IDX_DIM ** -0.5)
    qb_o[0] = (seg(OFF_QB, W_QB) * (HD_B ** -0.5)).astype(jnp.bfloat16)
    kb_o[0] = seg(OFF_KB, 2 * W_KVB).astype(jnp.bfloat16)
    vb_o[0] = seg(OFF_VB, 2 * W_KVB).astype(jnp.bfloat16)


def _proj(h, g, w, kvg, kg, kb):
    b, s, d = h.shape
    tm = min(PROJ_TOKEN_TILE, s)
    bf = jnp.bfloat16

    def tok(width, dtype):
        return (jax.ShapeDtypeStruct((b, s, width), dtype),
                pl.BlockSpec((1, tm, width), lambda bi, ti: (bi, ti, 0)))

    outs = [
        tok(W_QA, bf),
        tok(KV_RANK, bf),
        (jax.ShapeDtypeStruct((b, s // BLOCK, KV_RANK, BLOCK), bf),
         pl.BlockSpec((1, tm // BLOCK, KV_RANK, BLOCK), lambda bi, ti: (bi, ti, 0, 0))),
        tok(W_QI, bf),
        tok(2 * IDX_DIM, bf),
        tok(LANES, jnp.float32),
        tok(W_QB, bf),
        tok(2 * W_KVB, bf),
        tok(2 * W_KVB, bf),
    ]
    const = lambda bi, ti: (0, 0)
    return pl.pallas_call(
        _proj_kernel,
        grid=(b, s // tm),
        in_specs=[
            pl.BlockSpec((1, tm, d), lambda bi, ti: (bi, ti, 0)),
            pl.BlockSpec((1, d), const),
            pl.BlockSpec((d, D_IN_PAD), const, pipeline_mode=pl.Buffered(1)),
            pl.BlockSpec((1, KV_RANK), const),
            pl.BlockSpec((1, 2 * IDX_DIM), const),
            pl.BlockSpec((1, 2 * IDX_DIM), const),
        ],
        out_specs=[o[1] for o in outs],
        out_shape=[o[0] for o in outs],
        scratch_shapes=[pltpu.VMEM((tm, d), bf)],
        compiler_params=pltpu.CompilerParams(
            dimension_semantics=("parallel", "parallel"),
            vmem_limit_bytes=VMEM_LIMIT_BYTES),
        name="proj",
    )(h, g, w, kvg, kg, kb)


def _sortable_to_f32(u):
    k = u ^ jnp.int32(-2 ** 31)
    bits = k ^ ((k >> 31) & jnp.int32(0x7FFFFFFF))
    return lax.bitcast_convert_type(bits, jnp.float32)


def _dsa_kernel(qa_ref, qi_ref, wi_ref, kk_ref, ckv_ref, ckvt_ref, wuk_ref, wuv_ref, bias_ref,
                o_ref, score_ref, hi_ref, acc_ref, m_ref, l_ref, qlat_ref, *chain_refs,
                k_sel, tk, idx_bits, scan_tiles, n_scan_max):
    n_chains = H_A // ATTN_HEADS_PER_CHAIN
    chain_cols = ATTN_HEADS_PER_CHAIN * BLOCK
    x_refs, p_refs, alpha_refs = (chain_refs[k * n_chains:(k + 1) * n_chains] for k in range(3))
    i = pl.program_id(1)
    blocks_per_tile = tk // BLOCK
    n_tiles = i // blocks_per_tile + 1
    f32 = jnp.float32
    bf16 = jnp.bfloat16

    lane = lax.broadcasted_iota(jnp.int32, (BLOCK, LANES), 1)
    parts = []
    for h in range(IDX_HEADS):
        pair = qi_ref[0, :, (h // 2) * LANES:(h // 2 + 1) * LANES].astype(f32)
        keep = (lane < IDX_DIM) if h % 2 == 0 else (lane >= IDX_DIM)
        parts.append(jnp.where(keep, pair, 0.0).astype(bf16))
    qm = jnp.concatenate(parts, axis=0)
    wit = wi_ref[0].T

    t_pos = i * BLOCK + lax.broadcasted_iota(jnp.int32, (tk, LANES), 1)
    row = lax.broadcasted_iota(jnp.int32, (tk, LANES), 0)

    def score_body(kt, carry):
        base = pl.multiple_of(kt * tk, tk)
        keys = kk_ref[0, pl.ds(base, tk), :]
        sc = None
        for g in range(0, IDX_HEADS, SCORE_HEADS_PER_DOT):
            d = lax.dot_general(keys, qm[g * BLOCK:(g + SCORE_HEADS_PER_DOT) * BLOCK],
                                (((1,), (1,)), ((), ())), preferred_element_type=f32)
            for a in range(SCORE_HEADS_PER_DOT):
                term = jnp.maximum(d[:, a * BLOCK:(a + 1) * BLOCK], 0.0) * wit[g + a:g + a + 1, :]
                sc = term if sc is None else sc + term
        sc = jnp.where(base + row <= t_pos, sc, -jnp.inf)
        score_ref[pl.ds(base, tk), :] = sc
        near = sc.astype(HI_DTYPE).astype(f32)
        inward = lax.bitcast_convert_type(
            lax.bitcast_convert_type(near, jnp.int32) - BF16_ULP_BITS, f32)
        hi_ref[pl.ds(base, tk), :] = jnp.where(
            jnp.abs(near) > jnp.abs(sc), inward, near).astype(HI_DTYPE)
        return carry

    lax.fori_loop(0, n_tiles, score_body, 0)

    n_scan = (n_tiles + scan_tiles - 1) // scan_tiles
    scan_rows = scan_tiles * tk

    def pad_body(kt, carry):
        rows = pl.ds(pl.multiple_of(kt * tk, tk), tk)
        score_ref[rows, :] = jnp.full((tk, LANES), -jnp.inf, f32)
        hi_ref[rows, :] = jnp.full((tk, LANES), -jnp.inf, HI_DTYPE)
        return carry

    lax.fori_loop(n_tiles, n_scan * scan_tiles, pad_body, 0)

    acc_rows = COUNT_ACCUMULATORS * SUBLANES
    scan_row = lax.broadcasted_iota(jnp.int32, (scan_rows, LANES), 0)

    def count(pred):
        def body(c, acc):
            base = pl.multiple_of(c * scan_rows, scan_rows)
            hit = jnp.where(pred(score_ref[pl.ds(base, scan_rows), :], base), 1, 0)
            return acc + hit.reshape(scan_rows // acc_rows, acc_rows, LANES).sum(axis=0)
        acc = lax.fori_loop(0, n_scan, body, jnp.zeros((acc_rows, LANES), jnp.int32))
        return jnp.sum(acc, axis=0, keepdims=True)

    packed = 2 * SUBLANES
    hi_shape = (scan_rows // (packed * COUNT_ACCUMULATORS), COUNT_ACCUMULATORS, packed, LANES)
    assert n_scan_max * hi_shape[0] <= 256

    def count_hi(thr_hi):
        one = jnp.ones(hi_shape, HI_DTYPE)
        zero = jnp.zeros(hi_shape, HI_DTYPE)

        def body(c, acc):
            base = pl.multiple_of(c * scan_rows, scan_rows)
            blk = hi_ref[pl.ds(base, scan_rows), :].reshape(hi_shape)
            hit = jnp.where(blk >= thr_hi[None, None], one, zero)
            for r in range(hi_shape[0]):
                acc = acc + hit[r]
            return acc

        acc = lax.fori_loop(0, n_scan, body, jnp.zeros(hi_shape[1:], HI_DTYPE))
        return jnp.sum(acc.astype(f32).reshape(COUNT_ACCUMULATORS * packed, LANES),
                       axis=0, keepdims=True).astype(jnp.int32)

    def bit_body(count_at):
        def body(it, carry):
            u, cnt_u = carry
            cand = u | lax.shift_left(jnp.int32(1), 31 - it)
            cnt = count_at(_sortable_to_f32(cand))
            take = cnt >= k_sel
            return jnp.where(take, cand, u), jnp.where(take, cnt, cnt_u)
        return body

    def count_top_half(thr_c):
        bits = lax.bitcast_convert_type(thr_c, jnp.int32) & jnp.int32(-BF16_ULP_BITS)
        return count_hi(lax.bitcast_convert_type(
            jnp.broadcast_to(bits, (packed, LANES)), f32).astype(HI_DTYPE))

    zero_row = jnp.zeros((1, LANES), jnp.int32)
    carry = lax.fori_loop(0, 16, bit_body(count_top_half), (zero_row, zero_row))
    u, cnt_u = lax.fori_loop(
        16, 32, bit_body(lambda thr_c: count(lambda blk, base: blk >= thr_c)), carry)
    thr = _sortable_to_f32(u)
    enough = thr >= F32_LOWEST
    thr = jnp.where(enough, thr, F32_LOWEST)

    has_tie = jnp.max(jnp.where(enough & (cnt_u > k_sel), 1.0, 0.0)) > 0.5

    @pl.when(has_tie)
    def _():
        need = k_sel - count(lambda blk, base: blk > thr)

        def idx_body(it, p):
            cand = p | lax.shift_left(jnp.int32(1), idx_bits - 1 - it)
            cnt = count(lambda blk, base: (blk == thr) & (base + scan_row < cand))
            return jnp.where(cnt < need, cand, p)

        last = lax.fori_loop(0, idx_bits, idx_body, jnp.zeros((1, LANES), jnp.int32))

        def drop_body(c, carry):
            base = pl.multiple_of(c * scan_rows, scan_rows)
            blk = score_ref[pl.ds(base, scan_rows), :]
            score_ref[pl.ds(base, scan_rows), :] = jnp.where(
                (blk == thr) & (base + scan_row > last), -jnp.inf, blk)
            return carry

        lax.fori_loop(0, n_scan, drop_body, 0)

    def mask_body(c, carry):
        rows = pl.ds(pl.multiple_of(c * scan_rows, scan_rows), scan_rows)
        score_ref[rows, :] = jnp.where(score_ref[rows, :] >= thr, 0.0, NEG)
        return carry

    lax.fori_loop(0, n_scan, mask_body, 0)

    scale = HD_A ** -0.5 * LOG2E
    for h in range(H_A):
        hs = slice(h * HD_A, (h + 1) * HD_A)
        qlt = lax.dot_general(wuk_ref[:, hs], qa_ref[0, :, hs], (((1,), (1,)), ((), ())),
                              preferred_element_type=f32)
        g, a = divmod(h, ATTN_HEADS_PER_CHAIN)
        qlat_ref[g, :, a * BLOCK:(a + 1) * BLOCK] = (qlt * scale).astype(bf16)

    def tile_mask(rows):
        return jnp.concatenate([score_ref[rows, :]] * ATTN_HEADS_PER_CHAIN, axis=1)

    near_blk = jnp.maximum(i - 1, 0)
    near_base = pl.multiple_of(near_blk * BLOCK, BLOCK)
    bias_off = pl.multiple_of(jnp.where(i > 0, 0, BLOCK), BLOCK)
    near_rows = pl.ds(near_base, 2 * BLOCK)
    near_c = ckv_ref[0, near_rows, :]
    near_ct = jnp.concatenate([ckvt_ref[0, near_blk], ckvt_ref[0, near_blk + 1]], axis=1)
    near_mask = tile_mask(near_rows)
    for g in range(n_chains):
        x = (jnp.dot(near_c, qlat_ref[g], preferred_element_type=f32)
             + bias_ref[pl.ds(bias_off, 2 * BLOCK), g * chain_cols:(g + 1) * chain_cols] + near_mask)
        m_new = jnp.max(x, axis=0, keepdims=True)
        p = jnp.exp2(x - m_new)
        m_ref[g] = m_new
        l_ref[g] = jnp.sum(p, axis=0, keepdims=True)
        acc_ref[g] = jnp.dot(near_ct, p.astype(bf16), preferred_element_type=f32)
    score_ref[near_rows, :] = jnp.full((2 * BLOCK, LANES), NEG, f32)

    n_far = (i - 1 + blocks_per_tile - 1) // blocks_per_tile

    def logits_stage(kt, g):
        kt = jnp.minimum(kt, n_far - 1)
        rows = pl.ds(pl.multiple_of(kt * tk, tk), tk)
        x_refs[g][...] = jnp.dot(ckv_ref[0, rows, :], qlat_ref[g],
                                 preferred_element_type=f32) + tile_mask(rows)

    def softmax_stage(g):
        x = x_refs[g][...]
        m_old = m_ref[g]
        m_new = jnp.maximum(m_old, jnp.max(x, axis=0, keepdims=True))
        p = jnp.exp2(x - m_new)
        alpha = jnp.exp2(m_old - m_new)
        l_ref[g] = alpha * l_ref[g] + jnp.sum(p, axis=0, keepdims=True)
        m_ref[g] = m_new
        alpha_refs[g][...] = alpha
        p_refs[g][...] = p.astype(bf16)

    def values_stage(kt, g):
        kt = jnp.maximum(kt, 0)
        c_t = jnp.concatenate(
            [ckvt_ref[0, kt * blocks_per_tile + j] for j in range(blocks_per_tile)], axis=1)
        acc_ref[g] = alpha_refs[g][...] * acc_ref[g] + jnp.dot(
            c_t, p_refs[g][...], preferred_element_type=f32)

    @pl.when(n_far > 0)
    def _():
        p_refs[n_chains - 1][...] = jnp.zeros((tk, chain_cols), bf16)
        alpha_refs[n_chains - 1][...] = jnp.ones((1, chain_cols), f32)
        for g in range(PIPELINE_LOOKAHEAD):
            logits_stage(0, g)

        def tile_body(kt, carry):
            for g in range(n_chains):
                ahead, behind = g + PIPELINE_LOOKAHEAD, g - 1
                logits_stage(kt + ahead // n_chains, ahead % n_chains)
                softmax_stage(g)
                values_stage(kt + behind // n_chains, behind % n_chains)
            return carry

        lax.fori_loop(0, n_far, tile_body, 0)
        values_stage(n_far - 1, n_chains - 1)

    for h in range(H_A):
        g, a = divmod(h, ATTN_HEADS_PER_CHAIN)
        hs = slice(a * BLOCK, (a + 1) * BLOCK)
        o_lat = (acc_ref[g, :, hs] / l_ref[g, :, hs]).T.astype(bf16)
        o_ref[0, :, h * HD_A:(h + 1) * HD_A] = jnp.dot(
            o_lat, wuv_ref[:, h * HD_A:(h + 1) * HD_A],
            preferred_element_type=f32).astype(bf16)


def _dsa(qa, qi, wi, kk, ckv, ckvt, wuk, wuv, bias, tk):
    b, s, _ = qa.shape
    nb = s // BLOCK
    assert nb >= 2
    k_sel = min(TOPK_MAX, s // 4)
    idx_bits = max(1, int(math.ceil(math.log2(s))))
    chain_cols = ATTN_HEADS_PER_CHAIN * BLOCK
    n_chains = H_A // ATTN_HEADS_PER_CHAIN
    scan_tiles = SCAN_TILES if (s // tk) % SCAN_TILES == 0 else 1
    blk = lambda width: pl.BlockSpec((1, BLOCK, width), lambda bi, i: (bi, i, 0))
    full = lambda width: pl.BlockSpec((1, s, width), lambda bi, i: (bi, 0, 0))
    const = lambda bi, i: (0, 0)
    return pl.pallas_call(
        functools.partial(_dsa_kernel, k_sel=k_sel, tk=tk, idx_bits=idx_bits, scan_tiles=scan_tiles,
                          n_scan_max=s // (tk * scan_tiles)),
        grid=(b, nb),
        in_specs=[
            blk(W_QA), blk(W_QI), blk(LANES),
            full(2 * IDX_DIM), full(KV_RANK),
            pl.BlockSpec((1, nb, KV_RANK, BLOCK), lambda bi, i: (bi, 0, 0, 0)),
            pl.BlockSpec((KV_RANK, W_QA), const),
            pl.BlockSpec((KV_RANK, W_QA), const),
            pl.BlockSpec(bias.shape, const),
        ],
        out_specs=blk(W_QA),
        out_shape=jax.ShapeDtypeStruct((b, s, W_QA), jnp.bfloat16),
        scratch_shapes=[
            pltpu.VMEM((s, LANES), jnp.float32),
            pltpu.VMEM((s, LANES), HI_DTYPE),
            pltpu.VMEM((n_chains, KV_RANK, chain_cols), jnp.float32),
            pltpu.VMEM((n_chains, 1, chain_cols), jnp.float32),
            pltpu.VMEM((n_chains, 1, chain_cols), jnp.float32),
            pltpu.VMEM((n_chains, KV_RANK, chain_cols), jnp.bfloat16),
        ]
          + [pltpu.VMEM((tk, chain_cols), jnp.float32)] * n_chains
          + [pltpu.VMEM((tk, chain_cols), jnp.bfloat16)] * n_chains
          + [pltpu.VMEM((1, chain_cols), jnp.float32)] * n_chains,
        compiler_params=pltpu.CompilerParams(
            dimension_semantics=("parallel", "arbitrary"),
            vmem_limit_bytes=VMEM_LIMIT_BYTES),
        name="dsa",
    )(qa, qi, wi, kk, ckv, ckvt, wuk, wuv, bias)


def _swa_kernel(sink_ref, q_ref, kp_ref, kc_ref, vp_ref, vc_ref, bias_ref, o_ref):
    i = pl.program_id(1)
    f32 = jnp.float32
    bf16 = jnp.bfloat16
    lane = lax.broadcasted_iota(jnp.int32, (BLOCK, LANES), 1)
    col = lax.broadcasted_iota(jnp.int32, (BLOCK, 2 * BLOCK), 1)
    prev_ok = (col >= BLOCK) | (i > 0)
    k_all = jnp.concatenate([kp_ref[0], kc_ref[0]], axis=0)
    v_all = jnp.concatenate([vp_ref[0], vc_ref[0]], axis=0)
    for pair in range(H_B // 2):
        kvh = (2 * pair) // GQA_GROUP
        kd = k_all[:, kvh * LANES:(kvh + 1) * LANES]
        vd = v_all[:, kvh * LANES:(kvh + 1) * LANES]
        q2 = q_ref[0, :, pair * LANES:(pair + 1) * LANES].astype(f32)
        halves = []
        for sub in range(2):
            h = 2 * pair + sub
            keep = (lane < HD_B) if sub == 0 else (lane >= HD_B)
            qm = jnp.where(keep, q2, 0.0).astype(bf16)
            lg = lax.dot_general(qm, kd, (((1,), (1,)), ((), ())),
                                 preferred_element_type=f32) + bias_ref[h]
            lg = jnp.where(prev_ok, lg, NEG)
            sink = sink_ref[h]
            m = jnp.maximum(jnp.max(lg, axis=-1, keepdims=True), sink)
            p = jnp.exp(lg - m)
            denom = jnp.sum(p, axis=-1, keepdims=True) + jnp.exp(sink - m)
            halves.append(jnp.dot(p.astype(bf16), vd, preferred_element_type=f32) / denom)
        o_ref[0, :, pair * LANES:(pair + 1) * LANES] = jnp.where(
            lane < HD_B, halves[0], halves[1]).astype(bf16)


def _swa(sinks, qb, kb, vb, bias):
    b, s, _ = qb.shape
    nb = s // BLOCK
    cur = lambda width: pl.BlockSpec((1, BLOCK, width), lambda bi, i: (bi, i, 0))
    prev = lambda width: pl.BlockSpec((1, BLOCK, width), lambda bi, i: (bi, jnp.maximum(i - 1, 0), 0))
    return pl.pallas_call(
        _swa_kernel,
        grid=(b, nb),
        in_specs=[
            pl.BlockSpec(memory_space=pltpu.SMEM),
            cur(W_QB), prev(2 * W_KVB), cur(2 * W_KVB), prev(2 * W_KVB), cur(2 * W_KVB),
            pl.BlockSpec(bias.shape, lambda bi, i: (0, 0, 0)),
        ],
        out_specs=cur(W_QB),
        out_shape=jax.ShapeDtypeStruct((b, s, W_QB), jnp.bfloat16),
        compiler_params=pltpu.CompilerParams(
            dimension_semantics=("parallel", "parallel"),
            vmem_limit_bytes=VMEM_LIMIT_BYTES),
        name="swa",
    )(sinks, qb, kb, kb, vb, vb, bias)


def _out_kernel(h_ref, oa_ref, ob_ref, wa_ref, wb_ref, o_ref):
    o_ref[...] = (h_ref[...]
                  + jnp.dot(oa_ref[...], wa_ref[...], preferred_element_type=jnp.float32)
                  + jnp.dot(ob_ref[...], wb_ref[...], preferred_element_type=jnp.float32))


def _out_proj(h, oa, ob, w):
    n, d = h.shape
    tm = min(OUT_TOKEN_TILE, n)
    wa_rows, wb_rows = oa.shape[1], ob.shape[1]
    assert wa_rows == wb_rows and w.shape[0] == wa_rows + wb_rows
    return pl.pallas_call(
        _out_kernel,
        grid=(n // tm,),
        in_specs=[
            pl.BlockSpec((tm, d), lambda i: (i, 0)),
            pl.BlockSpec((tm, wa_rows), lambda i: (i, 0)),
            pl.BlockSpec((tm, wb_rows), lambda i: (i, 0)),
            pl.BlockSpec((wa_rows, d), lambda i: (0, 0), pipeline_mode=pl.Buffered(1)),
            pl.BlockSpec((wb_rows, d), lambda i: (1, 0), pipeline_mode=pl.Buffered(1)),
        ],
        out_specs=pl.BlockSpec((tm, d), lambda i: (i, 0)),
        out_shape=jax.ShapeDtypeStruct((n, d), jnp.float32),
        compiler_params=pltpu.CompilerParams(
            dimension_semantics=("parallel",),
            vmem_limit_bytes=VMEM_LIMIT_BYTES),
        name="out_proj",
    )(h, oa, ob, w, w)


def _t5_bucket_np(dist):
    n = np.maximum(dist, 0)
    max_exact = N_BUCKETS // 2
    nf = np.maximum(n, 1).astype(np.float32)
    large = max_exact + (np.log(nf / np.float32(max_exact)) / np.float32(math.log(MAX_DISTANCE / max_exact))
                         * np.float32(N_BUCKETS - max_exact)).astype(np.int32)
    large = np.minimum(large, N_BUCKETS - 1)
    return np.where(n < max_exact, n, large)


def _band_distance():
    q = np.arange(BLOCK)[:, None]
    kj = np.arange(2 * BLOCK)[None, :]
    return BLOCK + q - kj


def _band_bias_kernel(table_ref, bucket_ref, o_ref):
    h = pl.program_id(0)
    bucket = bucket_ref[...]
    out = jnp.zeros(bucket.shape, jnp.float32)
    for b in range(N_BUCKETS):
        out = jnp.where(bucket == b, table_ref[b, h], out)
    o_ref[0] = out


def _band_bias(rel_bias):
    n_heads = rel_bias.shape[1]
    bucket = jnp.asarray(_t5_bucket_np(_band_distance()).astype(np.int32))
    return pl.pallas_call(
        _band_bias_kernel,
        grid=(n_heads,),
        in_specs=[
            pl.BlockSpec(memory_space=pltpu.SMEM),
            pl.BlockSpec(bucket.shape, lambda h: (0, 0)),
        ],
        out_specs=pl.BlockSpec((1,) + bucket.shape, lambda h: (h, 0, 0)),
        out_shape=jax.ShapeDtypeStruct((n_heads,) + bucket.shape, jnp.float32),
        compiler_params=pltpu.CompilerParams(dimension_semantics=("parallel",)),
        name="band_bias",
    )(rel_bias, bucket)


def _layout_w_in(w):
    splits = np.cumsum([W_QA, KV_RANK, W_QI, IDX_DIM, IDX_HEADS, W_QB, W_KVB])
    qa, ckv, qi, ki, wi, qb, kb, vb = jnp.split(w, [int(c) for c in splits], axis=-1)
    d = w.shape[0]
    dup = lambda a: jnp.concatenate(
        [a[:, kv * HD_B:(kv + 1) * HD_B] for kv in range(KVH_B) for _ in range(2)], axis=-1)
    wi_pad = jnp.concatenate([wi, jnp.zeros((d, LANES - IDX_HEADS), w.dtype)], axis=-1)
    out = jnp.concatenate([qa, ckv, qi, ki, ki, wi_pad, qb, dup(kb), dup(vb)], axis=-1)
    assert out.shape[1] == D_IN_PAD
    return out.astype(jnp.bfloat16)


def _cast_kernel(w_ref, o_ref):
    o_ref[...] = w_ref[0].astype(jnp.bfloat16)


def _cast_bf16(w, layer):
    _, r, c = w.shape
    rows = min(CAST_ROW_TILE * 2, r)
    return pl.pallas_call(
        _cast_kernel,
        grid=(r // rows,),
        in_specs=[pl.BlockSpec((1, rows, c), lambda i: (layer, i, 0))],
        out_specs=pl.BlockSpec((rows, c), lambda i: (i, 0)),
        out_shape=jax.ShapeDtypeStruct((r, c), jnp.bfloat16),
        compiler_params=pltpu.CompilerParams(
            dimension_semantics=("parallel",), vmem_limit_bytes=VMEM_LIMIT_BYTES),
        name="cast_bf16",
    )(w)


def _tile_major_kernel(w_ref, o_ref):
    tf = o_ref.shape[2]
    for j in range(o_ref.shape[0]):
        o_ref[j] = w_ref[0, :, j * tf:(j + 1) * tf].astype(jnp.bfloat16)


def _tile_major(w, layer):
    _, d, f = w.shape
    tf = FFN_HIDDEN_TILE
    rows = CAST_ROW_TILE
    return pl.pallas_call(
        _tile_major_kernel,
        grid=(d // rows,),
        in_specs=[pl.BlockSpec((1, rows, f), lambda i: (layer, i, 0))],
        out_specs=pl.BlockSpec((f // tf, rows, tf), lambda i: (0, i, 0)),
        out_shape=jax.ShapeDtypeStruct((f // tf, d, tf), jnp.bfloat16),
        compiler_params=pltpu.CompilerParams(
            dimension_semantics=("parallel",), vmem_limit_bytes=VMEM_LIMIT_BYTES),
        name="tile_major",
    )(w)


def kernel(x, rel_bias, ffn1_norm, ffn1_gate, ffn1_up, ffn1_down, mix_norm, w_in, kv_norm,
           idx_k_norm_g, idx_k_norm_b, w_uk, w_uv, sinks, w_out, ffn2_norm, ffn2_gate,
           ffn2_up, ffn2_down, final_norm):
    b, s, d = x.shape
    depth = w_in.shape[0]
    bf = jnp.bfloat16
    tk = min(DSA_KEY_TILE, s)
    assert s % BLOCK == 0 and s % tk == 0

    band = _band_bias(rel_bias)
    dist = _band_distance()
    rel = (band[:H_A] - rel_bias[N_BUCKETS - 1, :H_A][:, None, None]) * LOG2E
    rel = jnp.where(jnp.asarray(dist >= 0)[None], rel, 0.0)
    rel = jnp.transpose(rel, (2, 0, 1)).reshape(2 * BLOCK, H_A * BLOCK)
    dsa_bias = jnp.concatenate([rel, jnp.zeros((BLOCK, H_A * BLOCK), jnp.float32)], axis=0)
    swa_bias = jnp.where(jnp.asarray((dist >= 0) & (dist < WINDOW))[None], band[H_A:], NEG)

    fin = final_norm.reshape(1, d)
    dup2 = lambda a: jnp.concatenate([a, a], axis=-1).reshape(1, 2 * IDX_DIM)

    h = x.reshape(b * s, d)
    for l in range(depth):
        h = _ffn(h, ffn1_norm[l].reshape(1, d), _tile_major(ffn1_gate, l), _tile_major(ffn1_up, l),
                 _cast_bf16(ffn1_down, l), fin, False)
        qa, ckv, ckvt, qi, kk, wi, qb, kb, vb = _proj(
            h.reshape(b, s, d), mix_norm[l].reshape(1, d), _layout_w_in(w_in[l]),
            kv_norm[l].reshape(1, KV_RANK), dup2(idx_k_norm_g[l]), dup2(idx_k_norm_b[l]))
        o_a = _dsa(qa, qi, wi, kk, ckv, ckvt,
                   w_uk[l].reshape(KV_RANK, W_QA).astype(bf),
                   w_uv[l].reshape(KV_RANK, W_QA).astype(bf), dsa_bias, tk)
        o_b = _swa(sinks[l], qb, kb, vb, swa_bias)
        h = _out_proj(h, o_a.reshape(b * s, W_QA), o_b.reshape(b * s, W_QB), _cast_bf16(w_out, l))
        h = _ffn(h, ffn2_norm[l].reshape(1, d), _tile_major(ffn2_gate, l), _tile_major(ffn2_up, l),
                 _cast_bf16(ffn2_down, l), fin, l == depth - 1)
    return h.reshape(b, s, d)
```
